```python
import jax, jax.numpy as jnp
from jax import lax
import numpy as np

D_MODEL = 4096
BATCH = 2
SEQ = 4096
DEPTH = 2

N_MIXERS = 2
D_FF = 11008
CONV_WIDTH = 31
HEAD_DIM = 128
HEADS_PER_GROUP = D_MODEL // HEAD_DIM
ATTN_GROUPS = ((128, 1), (512, 4), (2048, 16))
N_GROUPS = len(ATTN_GROUPS)
QKV_WIDTH = 3 * N_GROUPS * HEADS_PER_GROUP * HEAD_DIM
ROPE_THETA = 10000.0
RMS_EPS = 1e-6
LN_EPS = 1e-5
NEG_INF = -1e30

kernel_name = "hybrid_conformer_conv_dilated_attn_macaron"


def rms_norm(x, g):
    xf = x.astype(jnp.float32)
    y = xf * lax.rsqrt(jnp.mean(xf * xf, axis=-1, keepdims=True) + RMS_EPS)
    return (y * g.astype(jnp.float32)).astype(x.dtype)


def swiglu(x, wg, wu, wd):
    return (jax.nn.silu(x @ wg) * (x @ wu)) @ wd


def conformer_conv(x, pw1, pw1_b, dw, dw_b, ln_g, ln_b, pw2, pw2_b):
    h = x @ pw1 + pw1_b
    a, b = jnp.split(h, 2, axis=-1)
    h = a * jax.nn.sigmoid(b)
    c = h.shape[-1]
    h = lax.conv_general_dilated(
        h, dw[:, None, :], window_strides=(1,),
        padding=[(CONV_WIDTH // 2, CONV_WIDTH // 2)],
        dimension_numbers=('NWC', 'WIO', 'NWC'),
        feature_group_count=c) + dw_b
    hf = h.astype(jnp.float32)
    mu = jnp.mean(hf, axis=-1, keepdims=True)
    var = jnp.mean(jnp.square(hf - mu), axis=-1, keepdims=True)
    hf = (hf - mu) * lax.rsqrt(var + LN_EPS) * ln_g.astype(jnp.float32) + ln_b.astype(jnp.float32)
    h = jax.nn.silu(hf).astype(x.dtype)
    return h @ pw2 + pw2_b


def rope_tables(s):
    inv = ROPE_THETA ** (-jnp.arange(0, HEAD_DIM, 2, dtype=jnp.float32) / HEAD_DIM)
    ang = jnp.arange(s, dtype=jnp.float32)[:, None] * inv[None, :]
    return jnp.cos(ang), jnp.sin(ang)


def apply_rope(t, cos, sin):
    tf = t.astype(jnp.float32)
    t1, t2 = jnp.split(tf, 2, axis=-1)
    c = cos[None, :, None, None, :]
    s = sin[None, :, None, None, :]
    return jnp.concatenate([t1 * c - t2 * s, t2 * c + t1 * s], axis=-1).astype(t.dtype)


def dilated_band_attention(q, k, v, dilation, radius):
    b, s, h, dh = q.shape
    L = s // dilation
    nb = -(-L // radius)
    lp = nb * radius

    def to_classes(t):
        return t.reshape(b, L, dilation, h, dh).transpose(0, 2, 1, 3, 4)

    qc = jnp.pad(to_classes(q), ((0, 0), (0, 0), (0, lp - L), (0, 0), (0, 0)))
    kpad = ((0, 0), (0, 0), (radius, lp - L + radius), (0, 0), (0, 0))
    kc = jnp.pad(to_classes(k), kpad)
    vc = jnp.pad(to_classes(v), kpad)
    qb = qc.reshape(b, dilation, nb, radius, h, dh)

    def windows(t):
        tb = t.reshape(b, dilation, nb + 2, radius, h, dh)
        return jnp.concatenate([tb[:, :, :-2], tb[:, :, 1:-1], tb[:, :, 2:]], axis=3)

    kw, vw = windows(kc), windows(vc)
    blk = jnp.arange(nb)[:, None, None]
    qidx = blk * radius + jnp.arange(radius)[None, :, None]
    kidx = (blk - 1) * radius + jnp.arange(3 * radius)[None, None, :]
    mask = (jnp.abs(kidx - qidx) <= radius) & (kidx >= 0) & (kidx < L)

    sc = jnp.einsum('bcnqhd,bcnkhd->bcnhqk', qb, kw,
                    preferred_element_type=jnp.float32) * (dh ** -0.5)
    sc = jnp.where(mask[:, None], sc, NEG_INF)
    m = jnp.max(sc, axis=-1, keepdims=True)
    p = jnp.exp(sc - m)
    den = jnp.sum(p, axis=-1)
    o = jnp.einsum('bcnhqk,bcnkhd->bcnqhd', p, vw.astype(jnp.float32))
    o = o / jnp.swapaxes(den, -1, -2)[..., None]
    lse = jnp.swapaxes(m[..., 0] + jnp.log(den), -1, -2)

    def from_classes(t):
        t = t.reshape((b, dilation, lp) + t.shape[4:])[:, :, :L]
        t = jnp.moveaxis(t, 1, 2)
        return t.reshape((b, s) + t.shape[3:])

    return from_classes(o), from_classes(lse)


def dilated_attention_mixer(x, w_qkv, w_o, cos, sin):
    b, s, _ = x.shape
    qkv = (x @ w_qkv).reshape(b, s, N_GROUPS, 3, HEADS_PER_GROUP, HEAD_DIM)
    q = apply_rope(qkv[:, :, :, 0], cos, sin)
    k = apply_rope(qkv[:, :, :, 1], cos, sin)
    v = qkv[:, :, :, 2]
    outs, lses = [], []
    for g, (window, dilation) in enumerate(ATTN_GROUPS):
        o, l = dilated_band_attention(q[:, :, g], k[:, :, g], v[:, :, g],
                                      dilation, window // (2 * dilation))
        outs.append(o)
        lses.append(l)
    w = jax.nn.softmax(jnp.stack(lses, axis=0), axis=0)
    o = jnp.sum(w[..., None] * jnp.stack(outs, axis=0), axis=0)
    return o.reshape(b, s, HEADS_PER_GROUP * HEAD_DIM).astype(x.dtype) @ w_o


def setup_inputs(seed: int = 0) -> dict:
    key = jax.random.key(seed)
    ks = jax.random.split(key, 24)
    n_conv = len(range(0, DEPTH, N_MIXERS))
    n_attn = len(range(1, DEPTH, N_MIXERS))
    d, f = D_MODEL, D_FF
    attn_w = HEADS_PER_GROUP * HEAD_DIM

    def nrm(k, shape, scale):
        return scale * jax.random.normal(k, shape, dtype=jnp.float32)

    return {
        "x": nrm(ks[0], (BATCH, SEQ, d), 1.0),
        "ffn1_norm": 1.0 + nrm(ks[1], (DEPTH, d), 0.02),
        "ffn1_wg": nrm(ks[2], (DEPTH, d, f), d ** -0.5),
        "ffn1_wu": nrm(ks[3], (DEPTH, d, f), d ** -0.5),
        "ffn1_wd": nrm(ks[4], (DEPTH, f, d), f ** -0.5),
        "mix_norm": 1.0 + nrm(ks[5], (DEPTH, d), 0.02),
        "ffn2_norm": 1.0 + nrm(ks[6], (DEPTH, d), 0.02),
        "ffn2_wg": nrm(ks[7], (DEPTH, d, f), d ** -0.5),
        "ffn2_wu": nrm(ks[8], (DEPTH, d, f), d ** -0.5),
        "ffn2_wd": nrm(ks[9], (DEPTH, f, d), f ** -0.5),
        "conv_pw1": nrm(ks[10], (n_conv, d, 2 * d), d ** -0.5),
        "conv_pw1_b": nrm(ks[11], (n_conv, 2 * d), 0.02),
        "conv_dw": nrm(ks[12], (n_conv, CONV_WIDTH, d), CONV_WIDTH ** -0.5),
        "conv_dw_b": nrm(ks[13], (n_conv, d), 0.02),
        "conv_ln_g": 1.0 + nrm(ks[14], (n_conv, d), 0.02),
        "conv_ln_b": nrm(ks[15], (n_conv, d), 0.02),
        "conv_pw2": nrm(ks[16], (n_conv, d, d), d ** -0.5),
        "conv_pw2_b": nrm(ks[17], (n_conv, d), 0.02),
        "attn_wqkv": nrm(ks[18], (n_attn, d, QKV_WIDTH), d ** -0.5),
        "attn_wo": nrm(ks[19], (n_attn, attn_w, d), attn_w ** -0.5),
        "final_norm": 1.0 + nrm(ks[20], (d,), 0.02),
    }


def reference(x, ffn1_norm, ffn1_wg, ffn1_wu, ffn1_wd, mix_norm, ffn2_norm, ffn2_wg,
              ffn2_wu, ffn2_wd, conv_pw1, conv_pw1_b, conv_dw, conv_dw_b, conv_ln_g,
              conv_ln_b, conv_pw2, conv_pw2_b, attn_wqkv, attn_wo, final_norm):
    s = x.shape[1]
    cos, sin = rope_tables(s)
    for i in range(DEPTH):
        h = rms_norm(x, ffn1_norm[i])
        x = x + 0.5 * swiglu(h, ffn1_wg[i], ffn1_wu[i], ffn1_wd[i])
        h = rms_norm(x, mix_norm[i])
        j = i // N_MIXERS
        if i % N_MIXERS == 0:
            x = x + conformer_conv(h, conv_pw1[j], conv_pw1_b[j], conv_dw[j], conv_dw_b[j],
                                   conv_ln_g[j], conv_ln_b[j], conv_pw2[j], conv_pw2_b[j])
        else:
            x = x + dilated_attention_mixer(h, attn_wqkv[j], attn_wo[j], cos, sin)
        h = rms_norm(x, ffn2_norm[i])
        x = x + 0.5 * swiglu(h, ffn2_wg[i], ffn2_wu[i], ffn2_wd[i])
    return rms_norm(x, final_norm)
```

```python
import functools

import jax
import jax.numpy as jnp
from jax import lax
from jax.experimental import pallas as pl
from jax.experimental.pallas import tpu as pltpu

F32 = jnp.float32
BF16 = jnp.bfloat16

HEAD_DIM = 128
ATTN_GROUPS = ((128, 1), (512, 4), (2048, 16))
CONV_WIDTH = 31
ROPE_THETA = 10000.0
RMS_EPS = 1e-6
LN_EPS = 1e-5
NEG_INF = -1e30

V7X_LANES = 128
V7X_VMEM_LIMIT_BYTES = 56 * 1024 * 1024


def _params(*semantics):
    return pltpu.CompilerParams(
        dimension_semantics=semantics, vmem_limit_bytes=V7X_VMEM_LIMIT_BYTES)


def _pick(n, pref):
    t = min(n, pref)
    while n % t:
        t //= 2
    return t


def _rms_rows(x, gain):
    ms = jnp.mean(x * x, axis=-1, keepdims=True)
    return x * lax.rsqrt(ms + RMS_EPS) * gain


def _rmsnorm_kernel(x_ref, g_ref, o_ref):
    o_ref[...] = _rms_rows(x_ref[...], g_ref[...]).astype(o_ref.dtype)


def rmsnorm(x, gain, out_dtype):
    t, d = x.shape
    bm = _pick(t, 256)
    return pl.pallas_call(
        _rmsnorm_kernel,
        grid=(t // bm,),
        in_specs=[pl.BlockSpec((bm, d), lambda i: (i, 0)),
                  pl.BlockSpec((1, d), lambda i: (0, 0))],
        out_specs=pl.BlockSpec((bm, d), lambda i: (i, 0)),
        out_shape=jax.ShapeDtypeStruct((t, d), out_dtype),
        compiler_params=_params("parallel"),
        name="rmsnorm",
    )(x, gain.reshape(1, d))


def _ffn_kernel(x_ref, g_ref, wg_ref, wu_ref, wd_ref, o_ref, h_ref):
    @pl.when(pl.program_id(1) == 0)
    def _():
        x = x_ref[...]
        h_ref[...] = _rms_rows(x, g_ref[...]).astype(BF16)
        o_ref[...] = x

    h = h_ref[...]
    gate = jnp.dot(h, wg_ref[...], preferred_element_type=F32)
    up = jnp.dot(h, wu_ref[...], preferred_element_type=F32)
    act = (gate * jax.nn.sigmoid(gate) * up * 0.5).astype(BF16)
    o_ref[...] += jnp.dot(act, wd_ref[...], preferred_element_type=F32)


def ffn(x, gain, wg, wu, wd):
    t, d = x.shape
    f = wg.shape[1]
    bm = _pick(t, 512)
    bf = _pick(f, 256)
    return pl.pallas_call(
        _ffn_kernel,
        grid=(t // bm, f // bf),
        in_specs=[pl.BlockSpec((bm, d), lambda i, j: (i, 0), pipeline_mode=pl.Buffered(1)),
                  pl.BlockSpec((1, d), lambda i, j: (0, 0)),
                  pl.BlockSpec((d, bf), lambda i, j: (0, j)),
                  pl.BlockSpec((d, bf), lambda i, j: (0, j)),
                  pl.BlockSpec((bf, d), lambda i, j: (j, 0))],
        out_specs=pl.BlockSpec((bm, d), lambda i, j: (i, 0)),
        out_shape=jax.ShapeDtypeStruct((t, d), F32),
        scratch_shapes=[pltpu.VMEM((bm, d), BF16)],
        compiler_params=_params("parallel", "arbitrary"),
        name="ffn",
    )(x, gain.reshape(1, d), wg, wu, wd)


def _glu_kernel(h_ref, wa_ref, wb_ref, ba_ref, bb_ref, o_ref):
    h = h_ref[...]
    a = jnp.dot(h, wa_ref[...], preferred_element_type=F32) + ba_ref[...]
    b = jnp.dot(h, wb_ref[...], preferred_element_type=F32) + bb_ref[...]
    o_ref[...] = a * jax.nn.sigmoid(b)


def glu_proj(h, w, bias):
    t, d = h.shape
    n = w.shape[1] // 2
    bm = _pick(t, 1024)
    bn = _pick(n, 512)
    nb = n // bn
    bias = bias.reshape(1, 2 * n)
    return pl.pallas_call(
        _glu_kernel,
        grid=(t // bm, nb),
        in_specs=[pl.BlockSpec((bm, d), lambda i, j: (i, 0)),
                  pl.BlockSpec((d, bn), lambda i, j: (0, j)),
                  pl.BlockSpec((d, bn), lambda i, j: (0, j + nb)),
                  pl.BlockSpec((1, bn), lambda i, j: (0, j)),
                  pl.BlockSpec((1, bn), lambda i, j: (0, j + nb))],
        out_specs=pl.BlockSpec((bm, bn), lambda i, j: (i, j)),
        out_shape=jax.ShapeDtypeStruct((t, n), F32),
        compiler_params=_params("parallel", "arbitrary"),
        name="glu_proj",
    )(h, w, w, bias, bias)


def _res_proj_kernel(h_ref, w_ref, b_ref, x_ref, o_ref):
    acc = jnp.dot(h_ref[...], w_ref[...], preferred_element_type=F32)
    o_ref[...] = x_ref[...] + (acc + b_ref[...])


def res_proj(h, w, bias, x):
    t, k = h.shape
    n = w.shape[1]
    bm = _pick(t, 1024)
    bn = _pick(n, 512)
    return pl.pallas_call(
        _res_proj_kernel,
        grid=(t // bm, n // bn),
        in_specs=[pl.BlockSpec((bm, k), lambda i, j: (i, 0)),
                  pl.BlockSpec((k, bn), lambda i, j: (0, j)),
                  pl.BlockSpec((1, bn), lambda i, j: (0, j)),
                  pl.BlockSpec((bm, bn), lambda i, j: (i, j))],
        out_specs=pl.BlockSpec((bm, bn), lambda i, j: (i, j)),
        out_shape=jax.ShapeDtypeStruct((t, n), F32),
        compiler_params=_params("parallel", "arbitrary"),
        name="res_proj",
    )(h, w, bias.reshape(1, n), x)


def _qkv_kernel(h_ref, w_ref, cos_ref, sin_ref, o_ref, *, heads_per_tile, tiles_per_part):
    acc = jnp.dot(h_ref[...], w_ref[...], preferred_element_type=F32)
    part = (pl.program_id(1) // tiles_per_part) % 3
    rotate = part < 2
    cos = jnp.where(rotate, cos_ref[...], 1.0)
    sin = jnp.where(rotate, sin_ref[...], 0.0)
    for hh in range(heads_per_tile):
        cols = slice(hh * HEAD_DIM, (hh + 1) * HEAD_DIM)
        t = acc[:, cols]
        swapped = pltpu.roll(t, HEAD_DIM // 2, axis=1)
        o_ref[:, cols] = (t * cos + swapped * sin).astype(o_ref.dtype)


def qkv_proj(h, w, cos_full, sin_signed, seq, part_width):
    t, d = h.shape
    n = w.shape[1]
    bm = _pick(seq, 1024)
    bn = _pick(part_width, 512)
    pos_tiles = seq // bm
    kern = functools.partial(_qkv_kernel, heads_per_tile=bn // HEAD_DIM,
                             tiles_per_part=part_width // bn)
    return pl.pallas_call(
        kern,
        grid=(t // bm, n // bn),
        in_specs=[pl.BlockSpec((bm, d), lambda i, j: (i, 0)),
                  pl.BlockSpec((d, bn), lambda i, j: (0, j)),
                  pl.BlockSpec((bm, HEAD_DIM), lambda i, j: (i % pos_tiles, 0)),
                  pl.BlockSpec((bm, HEAD_DIM), lambda i, j: (i % pos_tiles, 0))],
        out_specs=pl.BlockSpec((bm, bn), lambda i, j: (i, j)),
        out_shape=jax.ShapeDtypeStruct((t, n), BF16),
        compiler_params=_params("parallel", "arbitrary"),
        name="qkv_proj",
    )(h, w, cos_full, sin_signed)


CONV_HALO = 16
CONV_ROWS = 64
CONV_COLS = 256


def _conv_kernel(prev_ref, cur_ref, next_ref, dw_ref, dwb_ref, lng_ref, lnb_ref,
                 o_ref, xs_ref, cv_ref):
    ts, d = cur_ref.shape[1], cur_ref.shape[2]
    i = pl.program_id(1)
    last = pl.num_programs(1) - 1
    xs_ref[0:CONV_HALO, :] = jnp.where(i > 0, prev_ref[0], 0.0)
    xs_ref[CONV_HALO:CONV_HALO + ts, :] = cur_ref[0]
    xs_ref[CONV_HALO + ts:, :] = jnp.where(i < last, next_ref[0], 0.0)

    n_rows = ts // CONV_ROWS
    n_cols = d // CONV_COLS
    shift = CONV_HALO - CONV_WIDTH // 2

    def chunk(idx, carry):
        cols = pl.ds(pl.multiple_of(idx * CONV_COLS, CONV_COLS), CONV_COLS)
        for r0 in range(0, ts, CONV_ROWS):
            acc = jnp.zeros((CONV_ROWS, CONV_COLS), F32)
            for k in range(CONV_WIDTH):
                taps = xs_ref[r0 + k + shift:r0 + k + shift + CONV_ROWS, cols]
                acc = acc + taps * dw_ref[k:k + 1, cols]
            cv_ref[r0:r0 + CONV_ROWS, cols] = acc + dwb_ref[:, cols]
        return carry

    lax.fori_loop(0, n_cols, chunk, 0)

    def norm(r, carry):
        r0 = pl.multiple_of(r * CONV_ROWS, CONV_ROWS)
        hf = cv_ref[pl.ds(r0, CONV_ROWS), :]
        mu = jnp.mean(hf, axis=-1, keepdims=True)
        cen = hf - mu
        var = jnp.mean(cen * cen, axis=-1, keepdims=True)
        y = cen * lax.rsqrt(var + LN_EPS) * lng_ref[...] + lnb_ref[...]
        o_ref[0, pl.ds(r0, CONV_ROWS), :] = (y * jax.nn.sigmoid(y)).astype(o_ref.dtype)
        return carry

    lax.fori_loop(0, n_rows, norm, 0)


def conv_core(x, dw, dw_b, ln_g, ln_b):
    b, s, d = x.shape
    ts = _pick(s, 256)
    hb = ts // CONV_HALO
    n_halo_blocks = s // CONV_HALO
    row = lambda v: v.reshape(1, d)
    return pl.pallas_call(
        _conv_kernel,
        grid=(b, s // ts),
        in_specs=[
            pl.BlockSpec((1, CONV_HALO, d), lambda bi, i: (bi, jnp.maximum(i * hb - 1, 0), 0)),
            pl.BlockSpec((1, ts, d), lambda bi, i: (bi, i, 0)),
            pl.BlockSpec((1, CONV_HALO, d),
                         lambda bi, i: (bi, jnp.minimum((i + 1) * hb, n_halo_blocks - 1), 0)),
            pl.BlockSpec((CONV_WIDTH, d), lambda bi, i: (0, 0)),
            pl.BlockSpec((1, d), lambda bi, i: (0, 0)),
            pl.BlockSpec((1, d), lambda bi, i: (0, 0)),
            pl.BlockSpec((1, d), lambda bi, i: (0, 0)),
        ],
        out_specs=pl.BlockSpec((1, ts, d), lambda bi, i: (bi, i, 0)),
        out_shape=jax.ShapeDtypeStruct((b, s, d), BF16),
        scratch_shapes=[pltpu.VMEM((ts + 2 * CONV_HALO, d), F32),
                        pltpu.VMEM((ts, d), F32)],
        compiler_params=_params("parallel", "arbitrary"),
        name="conv_core",
    )(x, x, x, dw, row(dw_b), row(ln_g), row(ln_b))


ATTN_HEADS_PER_STEP = 4


def _band_attn_kernel(q_ref, kp_ref, kc_ref, kn_ref, vp_ref, vc_ref, vn_ref,
                      o_ref, lse_ref, *, radius, length):
    bq = q_ref.shape[1]
    nk = bq + 2 * radius
    j0 = pl.program_id(3) * bq
    qi = lax.broadcasted_iota(jnp.int32, (bq, nk), 0)
    kj = lax.broadcasted_iota(jnp.int32, (bq, nk), 1) - radius
    kabs = kj + j0
    valid = (jnp.abs(kj - qi) <= radius) & (kabs >= 0) & (kabs < length)
    scale = HEAD_DIM ** -0.5
    for hh in range(q_ref.shape[2] // HEAD_DIM):
        cols = slice(hh * HEAD_DIM, (hh + 1) * HEAD_DIM)
        q = q_ref[0, :, cols]
        keys = jnp.concatenate([kp_ref[0, :, cols], kc_ref[0, :, cols], kn_ref[0, :, cols]], axis=0)
        vals = jnp.concatenate([vp_ref[0, :, cols], vc_ref[0, :, cols], vn_ref[0, :, cols]], axis=0)
        sc = lax.dot_general(q, keys, (((1,), (1,)), ((), ())),
                             preferred_element_type=F32) * scale
        sc = jnp.where(valid, sc, NEG_INF)
        m = jnp.max(sc, axis=-1, keepdims=True)
        p = jnp.exp(sc - m)
        den = jnp.sum(p, axis=-1, keepdims=True)
        o = jnp.dot(p.astype(BF16), vals, preferred_element_type=F32) / den
        o_ref[0, :, cols] = o
        lse_ref[0, :, cols] = jnp.broadcast_to(m + jnp.log(den), (bq, HEAD_DIM))


def band_attention(qkv, group, dilation, radius, batch, seq, n_heads):
    width = n_heads * HEAD_DIM
    total = qkv.shape[1]
    length = seq // dilation
    view = qkv.reshape(batch, length, dilation * total)
    w = min(ATTN_HEADS_PER_STEP, n_heads) * HEAD_DIM
    bq = _pick(length, 256)
    assert bq % radius == 0 and length % radius == 0
    rb = bq // radius
    n_rblocks = length // radius
    hblocks = width // w
    cols_per_class = total // w

    def col(part):
        base = (group * 3 + part) * hblocks
        return lambda b, c, h, i: c * cols_per_class + base + h

    def cur(part):
        cm = col(part)
        return pl.BlockSpec((1, bq, w), lambda b, c, h, i: (b, i, cm(b, c, h, i)))

    def prev(part):
        cm = col(part)
        return pl.BlockSpec((1, radius, w),
                            lambda b, c, h, i: (b, jnp.maximum(i * rb - 1, 0), cm(b, c, h, i)))

    def nxt(part):
        cm = col(part)
        return pl.BlockSpec(
            (1, radius, w),
            lambda b, c, h, i: (b, jnp.minimum((i + 1) * rb, n_rblocks - 1), cm(b, c, h, i)))

    out_spec = pl.BlockSpec((1, bq, w), lambda b, c, h, i: (b, i, c * hblocks + h))
    kern = functools.partial(_band_attn_kernel, radius=radius, length=length)
    o, lse = pl.pallas_call(
        kern,
        grid=(batch, dilation, hblocks, length // bq),
        in_specs=[cur(0), prev(1), cur(1), nxt(1), prev(2), cur(2), nxt(2)],
        out_specs=[out_spec, out_spec],
        out_shape=[jax.ShapeDtypeStruct((batch, length, dilation * width), F32)] * 2,
        compiler_params=_params("parallel", "parallel", "parallel", "arbitrary"),
        name=f"band_attn_d{dilation}",
    )(view, view, view, view, view, view, view)
    return o.reshape(batch * seq, width), lse.reshape(batch * seq, width)


def _combine_kernel(*refs):
    n = (len(refs) - 1) // 2
    o_refs, l_refs, out_ref = refs[:n], refs[n:2 * n], refs[2 * n]
    lses = [r[...] for r in l_refs]
    m = functools.reduce(jnp.maximum, lses)
    ws = [jnp.exp(l - m) for l in lses]
    den = functools.reduce(lambda a, b: a + b, ws)
    num = functools.reduce(lambda a, b: a + b, [w * r[...] for w, r in zip(ws, o_refs)])
    out_ref[...] = (num / den).astype(out_ref.dtype)


def combine_groups(outs, lses):
    t, width = outs[0].shape
    bm = _pick(t, 256)
    spec = pl.BlockSpec((bm, width), lambda i: (i, 0))
    return pl.pallas_call(
        _combine_kernel,
        grid=(t // bm,),
        in_specs=[spec] * (2 * len(outs)),
        out_specs=spec,
        out_shape=jax.ShapeDtypeStruct((t, width), BF16),
        compiler_params=_params("parallel"),
        name="combine_groups",
    )(*outs, *lses)


def _rope_tables(seq):
    inv = ROPE_THETA ** (-jnp.arange(0, HEAD_DIM, 2, dtype=F32) / HEAD_DIM)
    ang = jnp.arange(seq, dtype=F32)[:, None] * inv[None, :]
    cos, sin = jnp.cos(ang), jnp.sin(ang)
    return jnp.concatenate([cos, cos], axis=-1), jnp.concatenate([-sin, sin], axis=-1)


def kernel(x, ffn1_norm, ffn1_wg, ffn1_wu, ffn1_wd, mix_norm, ffn2_norm, ffn2_wg, ffn2_wu, ffn2_wd, conv_pw1, conv_pw1_b, conv_dw, conv_dw_b, conv_ln_g, conv_ln_b, conv_pw2, conv_pw2_b, attn_wqkv, attn_wo, final_norm):
    batch, seq, d = x.shape
    depth = ffn1_norm.shape[0]
    n_mixers = 2
    n_heads = attn_wo.shape[1] // HEAD_DIM
    cos_full, sin_signed = _rope_tables(seq)
    bf = lambda w: w.astype(BF16)

    xt = x.reshape(batch * seq, d)
    for i in range(depth):
        xt = ffn(xt, ffn1_norm[i], bf(ffn1_wg[i]), bf(ffn1_wu[i]), bf(ffn1_wd[i]))
        h = rmsnorm(xt, mix_norm[i], BF16)
        j = i // n_mixers
        if i % n_mixers == 0:
            glu = glu_proj(h, bf(conv_pw1[j]), conv_pw1_b[j])
            act = conv_core(glu.reshape(batch, seq, d), conv_dw[j], conv_dw_b[j],
                            conv_ln_g[j], conv_ln_b[j])
            xt = res_proj(act.reshape(batch * seq, d), bf(conv_pw2[j]), conv_pw2_b[j], xt)
        else:
            qkv = qkv_proj(h, bf(attn_wqkv[j]), cos_full, sin_signed, seq, n_heads * HEAD_DIM)
            outs, lses = [], []
            for g, (window, dilation) in enumerate(ATTN_GROUPS):
                o, l = band_attention(qkv, g, dilation, window // (2 * dilation),
                                      batch, seq, n_heads)
                outs.append(o)
                lses.append(l)
            mixed = combine_groups(outs, lses)
            xt = res_proj(mixed, bf(attn_wo[j]), jnp.zeros((d,), F32), xt)
        xt = ffn(xt, ffn2_norm[i], bf(ffn2_wg[i]), bf(ffn2_wu[i]), bf(ffn2_wd[i]))
    return rmsnorm(xt, final_norm, F32).reshape(batch, seq, d)
```

```python
import functools

import jax
import jax.numpy as jnp
from jax import lax
from jax.experimental import pallas as pl
from jax.experimental.pallas import tpu as pltpu

F32 = jnp.float32
BF16 = jnp.bfloat16

HEAD_DIM = 128
ATTN_GROUPS = ((128, 1), (512, 4), (2048, 16))
CONV_WIDTH = 31
ROPE_THETA = 10000.0
RMS_EPS = 1e-6
LN_EPS = 1e-5
NEG_INF = -1e30

V7X_LANES = 128
V7X_VMEM_LIMIT_BYTES = 56 * 1024 * 1024


def _params(*semantics):
    return pltpu.CompilerParams(
        dimension_semantics=semantics, vmem_limit_bytes=V7X_VMEM_LIMIT_BYTES)


def _pick(n, pref):
    if n <= pref:
        return n
    t = 1 << (pref.bit_length() - 1)
    while n % t:
        t //= 2
    return t


def _rms_rows(x, gain):
    ms = jnp.mean(x * x, axis=-1, keepdims=True)
    return x * lax.rsqrt(ms + RMS_EPS) * gain


def _rmsnorm_kernel(x_ref, g_ref, o_ref):
    o_ref[...] = _rms_rows(x_ref[...], g_ref[...]).astype(o_ref.dtype)


def rmsnorm(x, gain, out_dtype):
    t, d = x.shape
    bm = _pick(t, 256)
    return pl.pallas_call(
        _rmsnorm_kernel,
        grid=(t // bm,),
        in_specs=[pl.BlockSpec((bm, d), lambda i: (i, 0)),
                  pl.BlockSpec((1, d), lambda i: (0, 0))],
        out_specs=pl.BlockSpec((bm, d), lambda i: (i, 0)),
        out_shape=jax.ShapeDtypeStruct((t, d), out_dtype),
        compiler_params=_params("parallel"),
        name="rmsnorm",
    )(x, gain.reshape(1, d))


def _ffn_kernel(x_ref, g_ref, wg_ref, wu_ref, wd_ref, o_ref, h_ref):
    @pl.when(pl.program_id(1) == 0)
    def _():
        x = x_ref[...]
        h_ref[...] = _rms_rows(x, g_ref[...]).astype(BF16)
        o_ref[...] = x

    h = h_ref[...]
    gate = jnp.dot(h, wg_ref[...], preferred_element_type=F32)
    up = jnp.dot(h, wu_ref[...], preferred_element_type=F32)
    act = (gate * jax.nn.sigmoid(gate) * up * 0.5).astype(BF16)
    o_ref[...] += jnp.dot(act, wd_ref[...], preferred_element_type=F32)


def ffn(x, gain, wg, wu, wd):
    t, d = x.shape
    f = wg.shape[1]
    bm = _pick(t, 512)
    bf = _pick(f, 256)
    return pl.pallas_call(
        _ffn_kernel,
        grid=(t // bm, f // bf),
        in_specs=[pl.BlockSpec((bm, d), lambda i, j: (i, 0), pipeline_mode=pl.Buffered(1)),
                  pl.BlockSpec((1, d), lambda i, j: (0, 0)),
                  pl.BlockSpec((d, bf), lambda i, j: (0, j)),
                  pl.BlockSpec((d, bf), lambda i, j: (0, j)),
                  pl.BlockSpec((bf, d), lambda i, j: (j, 0))],
        out_specs=pl.BlockSpec((bm, d), lambda i, j: (i, 0)),
        out_shape=jax.ShapeDtypeStruct((t, d), F32),
        scratch_shapes=[pltpu.VMEM((bm, d), BF16)],
        compiler_params=_params("parallel", "arbitrary"),
        name="ffn",
    )(x, gain.reshape(1, d), wg, wu, wd)


def _glu_kernel(h_ref, wa_ref, wb_ref, ba_ref, bb_ref, o_ref):
    h = h_ref[...]
    a = jnp.dot(h, wa_ref[...], preferred_element_type=F32) + ba_ref[...]
    b = jnp.dot(h, wb_ref[...], preferred_element_type=F32) + bb_ref[...]
    o_ref[...] = a * jax.nn.sigmoid(b)


def glu_proj(h, w, bias):
    t, d = h.shape
    n = w.shape[1] // 2
    bm = _pick(t, 1024)
    bn = _pick(n, 512)
    nb = n // bn
    bias = bias.reshape(1, 2 * n)
    return pl.pallas_call(
        _glu_kernel,
        grid=(t // bm, nb),
        in_specs=[pl.BlockSpec((bm, d), lambda i, j: (i, 0)),
                  pl.BlockSpec((d, bn), lambda i, j: (0, j)),
                  pl.BlockSpec((d, bn), lambda i, j: (0, j + nb)),
                  pl.BlockSpec((1, bn), lambda i, j: (0, j)),
                  pl.BlockSpec((1, bn), lambda i, j: (0, j + nb))],
        out_specs=pl.BlockSpec((bm, bn), lambda i, j: (i, j)),
        out_shape=jax.ShapeDtypeStruct((t, n), F32),
        compiler_params=_params("parallel", "arbitrary"),
        name="glu_proj",
    )(h, w, w, bias, bias)


def _res_proj_kernel(h_ref, w_ref, b_ref, x_ref, o_ref):
    acc = jnp.dot(h_ref[...], w_ref[...], preferred_element_type=F32)
    o_ref[...] = x_ref[...] + (acc + b_ref[...])


def res_proj(h, w, bias, x):
    t, k = h.shape
    n = w.shape[1]
    bm = _pick(t, 1024)
    bn = _pick(n, 512)
    return pl.pallas_call(
        _res_proj_kernel,
        grid=(t // bm, n // bn),
        in_specs=[pl.BlockSpec((bm, k), lambda i, j: (i, 0)),
                  pl.BlockSpec((k, bn), lambda i, j: (0, j)),
                  pl.BlockSpec((1, bn), lambda i, j: (0, j)),
                  pl.BlockSpec((bm, bn), lambda i, j: (i, j))],
        out_specs=pl.BlockSpec((bm, bn), lambda i, j: (i, j)),
        out_shape=jax.ShapeDtypeStruct((t, n), F32),
        compiler_params=_params("parallel", "arbitrary"),
        name="res_proj",
    )(h, w, bias.reshape(1, n), x)


def _qkv_kernel(h_ref, w_ref, cos_ref, sin_ref, o_ref, *scratch, dilation, tiles_per_part):
    acc = jnp.dot(h_ref[...], w_ref[...], preferred_element_type=F32)
    bm, bn = acc.shape
    rotate = pl.program_id(1) // tiles_per_part < 2
    cos = jnp.where(rotate, cos_ref[...], 1.0)
    sin = jnp.where(rotate, sin_ref[...], 0.0)
    for hh in range(bn // HEAD_DIM):
        cols = slice(hh * HEAD_DIM, (hh + 1) * HEAD_DIM)
        t = acc[:, cols]
        val = t * cos + pltpu.roll(t, HEAD_DIM // 2, axis=1) * sin
        if dilation == 1:
            o_ref[0, :, cols] = val.astype(o_ref.dtype)
        else:
            scratch[0][hh] = val
    if dilation > 1:
        rows = bm // dilation
        for c in range(dilation):
            for hh in range(bn // HEAD_DIM):
                cols = slice(hh * HEAD_DIM, (hh + 1) * HEAD_DIM)
                o_ref[c, :, cols] = scratch[0][hh, pl.ds(c, rows, stride=dilation), :].astype(o_ref.dtype)


def qkv_proj(h, w, group, dilation, cos_full, sin_signed, batch, seq, part_width):
    t, d = h.shape
    n = 3 * part_width
    bm = _pick(seq, 1024)
    bn = _pick(part_width, 512)
    pos_tiles = seq // bm
    col_tiles = n // bn
    length = seq // dilation
    kern = functools.partial(_qkv_kernel, dilation=dilation, tiles_per_part=part_width // bn)
    scratch = [] if dilation == 1 else [pltpu.VMEM((bn // HEAD_DIM, bm, HEAD_DIM), F32)]
    return pl.pallas_call(
        kern,
        grid=(t // bm, col_tiles),
        in_specs=[pl.BlockSpec((bm, d), lambda i, j: (i, 0)),
                  pl.BlockSpec((d, bn), lambda i, j: (0, group * col_tiles + j)),
                  pl.BlockSpec((bm, HEAD_DIM), lambda i, j: (i % pos_tiles, 0)),
                  pl.BlockSpec((bm, HEAD_DIM), lambda i, j: (i % pos_tiles, 0))],
        out_specs=pl.BlockSpec((None, dilation, bm // dilation, bn),
                               lambda i, j: (i // pos_tiles, 0, i % pos_tiles, j)),
        out_shape=jax.ShapeDtypeStruct((batch, dilation, length, n), BF16),
        scratch_shapes=scratch,
        compiler_params=_params("parallel", "arbitrary"),
        name=f"qkv_proj_d{dilation}",
    )(h, w, cos_full, sin_signed)


CONV_HALO = 16
CONV_ROWS = 64
CONV_COLS = 256


def _conv_kernel(prev_ref, cur_ref, next_ref, dw_ref, dwb_ref, lng_ref, lnb_ref,
                 o_ref, xs_ref, cv_ref):
    ts, d = cur_ref.shape[1], cur_ref.shape[2]
    i = pl.program_id(1)
    last = pl.num_programs(1) - 1
    xs_ref[0:CONV_HALO, :] = jnp.where(i > 0, prev_ref[0], 0.0)
    xs_ref[CONV_HALO:CONV_HALO + ts, :] = cur_ref[0]
    xs_ref[CONV_HALO + ts:, :] = jnp.where(i < last, next_ref[0], 0.0)

    n_rows = ts // CONV_ROWS
    n_cols = d // CONV_COLS
    shift = CONV_HALO - CONV_WIDTH // 2

    def chunk(idx, carry):
        cols = pl.ds(pl.multiple_of(idx * CONV_COLS, CONV_COLS), CONV_COLS)
        for r0 in range(0, ts, CONV_ROWS):
            acc = jnp.zeros((CONV_ROWS, CONV_COLS), F32)
            for k in range(CONV_WIDTH):
                taps = xs_ref[r0 + k + shift:r0 + k + shift + CONV_ROWS, cols]
                acc = acc + taps * dw_ref[k:k + 1, cols]
            cv_ref[r0:r0 + CONV_ROWS, cols] = acc + dwb_ref[:, cols]
        return carry

    lax.fori_loop(0, n_cols, chunk, 0)

    def norm(r, carry):
        r0 = pl.multiple_of(r * CONV_ROWS, CONV_ROWS)
        hf = cv_ref[pl.ds(r0, CONV_ROWS), :]
        mu = jnp.mean(hf, axis=-1, keepdims=True)
        cen = hf - mu
        var = jnp.mean(cen * cen, axis=-1, keepdims=True)
        y = cen * lax.rsqrt(var + LN_EPS) * lng_ref[...] + lnb_ref[...]
        o_ref[0, pl.ds(r0, CONV_ROWS), :] = (y * jax.nn.sigmoid(y)).astype(o_ref.dtype)
        return carry

    lax.fori_loop(0, n_rows, norm, 0)


def conv_core(x, dw, dw_b, ln_g, ln_b):
    b, s, d = x.shape
    ts = _pick(s, 256)
    hb = ts // CONV_HALO
    n_halo_blocks = s // CONV_HALO
    row = lambda v: v.reshape(1, d)
    return pl.pallas_call(
        _conv_kernel,
        grid=(b, s // ts),
        in_specs=[
            pl.BlockSpec((1, CONV_HALO, d), lambda bi, i: (bi, jnp.maximum(i * hb - 1, 0), 0)),
            pl.BlockSpec((1, ts, d), lambda bi, i: (bi, i, 0)),
            pl.BlockSpec((1, CONV_HALO, d),
                         lambda bi, i: (bi, jnp.minimum((i + 1) * hb, n_halo_blocks - 1), 0)),
            pl.BlockSpec((CONV_WIDTH, d), lambda bi, i: (0, 0)),
            pl.BlockSpec((1, d), lambda bi, i: (0, 0)),
            pl.BlockSpec((1, d), lambda bi, i: (0, 0)),
            pl.BlockSpec((1, d), lambda bi, i: (0, 0)),
        ],
        out_specs=pl.BlockSpec((1, ts, d), lambda bi, i: (bi, i, 0)),
        out_shape=jax.ShapeDtypeStruct((b, s, d), BF16),
        scratch_shapes=[pltpu.VMEM((ts + 2 * CONV_HALO, d), F32),
                        pltpu.VMEM((ts, d), F32)],
        compiler_params=_params("parallel", "arbitrary"),
        name="conv_core",
    )(x, x, x, dw, row(dw_b), row(ln_g), row(ln_b))


ATTN_HEADS_PER_STEP = 4


def _band_attn_kernel(q_ref, kp_ref, kc_ref, kn_ref, vp_ref, vc_ref, vn_ref,
                      o_ref, lse_ref, *, radius, length):
    bq = q_ref.shape[0]
    nk = bq + 2 * radius
    j0 = pl.program_id(3) * bq
    qi = lax.broadcasted_iota(jnp.int32, (bq, nk), 0)
    kj = lax.broadcasted_iota(jnp.int32, (bq, nk), 1) - radius
    kabs = kj + j0
    valid = (jnp.abs(kj - qi) <= radius) & (kabs >= 0) & (kabs < length)
    scale = HEAD_DIM ** -0.5
    for hh in range(q_ref.shape[1] // HEAD_DIM):
        cols = slice(hh * HEAD_DIM, (hh + 1) * HEAD_DIM)
        q = q_ref[:, cols]
        keys = jnp.concatenate([kp_ref[:, cols], kc_ref[:, cols], kn_ref[:, cols]], axis=0)
        vals = jnp.concatenate([vp_ref[:, cols], vc_ref[:, cols], vn_ref[:, cols]], axis=0)
        sc = lax.dot_general(q, keys, (((1,), (1,)), ((), ())),
                             preferred_element_type=F32) * scale
        sc = jnp.where(valid, sc, NEG_INF)
        m = jnp.max(sc, axis=-1, keepdims=True)
        p = jnp.exp(sc - m)
        den = jnp.sum(p, axis=-1, keepdims=True)
        o = jnp.dot(p.astype(BF16), vals, preferred_element_type=F32) / den
        o_ref[:, cols] = o
        lse_ref[:, cols] = jnp.broadcast_to(m + jnp.log(den), (bq, HEAD_DIM))


def band_attention(qkv, dilation, radius):
    batch, _, length, total = qkv.shape
    width = total // 3
    w = min(ATTN_HEADS_PER_STEP * HEAD_DIM, width)
    bq = _pick(length, 256)
    assert bq % radius == 0 and length % radius == 0
    rb = bq // radius
    n_rblocks = length // radius
    hblocks = width // w

    def cur(part):
        return pl.BlockSpec((None, None, bq, w), lambda b, c, h, i: (b, c, i, part * hblocks + h))

    def prev(part):
        return pl.BlockSpec((None, None, radius, w),
                            lambda b, c, h, i: (b, c, jnp.maximum(i * rb - 1, 0), part * hblocks + h))

    def nxt(part):
        return pl.BlockSpec(
            (None, None, radius, w),
            lambda b, c, h, i: (b, c, jnp.minimum((i + 1) * rb, n_rblocks - 1), part * hblocks + h))

    out_spec = pl.BlockSpec((None, None, bq, w), lambda b, c, h, i: (b, c, i, h))
    kern = functools.partial(_band_attn_kernel, radius=radius, length=length)
    return pl.pallas_call(
        kern,
        grid=(batch, dilation, hblocks, length // bq),
        in_specs=[cur(0), prev(1), cur(1), nxt(1), prev(2), cur(2), nxt(2)],
        out_specs=[out_spec, out_spec],
        out_shape=[jax.ShapeDtypeStruct((batch, dilation, length, width), F32)] * 2,
        compiler_params=_params("parallel", "parallel", "parallel", "arbitrary"),
        name=f"band_attn_d{dilation}",
    )(qkv, qkv, qkv, qkv, qkv, qkv, qkv)


COMBINE_ROWS = 256
COMBINE_COLS = 1024


def _combine_kernel(*refs, dilations):
    n = len(dilations)
    o_refs, l_refs, out_ref, scr_ref = refs[:n], refs[n:2 * n], refs[2 * n], refs[2 * n + 1]
    bm, bn = out_ref.shape
    slabs = bn // V7X_LANES
    for g, d in enumerate(dilations):
        for a, ref in enumerate((o_refs[g], l_refs[g])):
            for c in range(d):
                for s in range(slabs):
                    lanes = slice(s * V7X_LANES, (s + 1) * V7X_LANES)
                    rows = pl.ds(c, bm // d, stride=d) if d > 1 else slice(None)
                    scr_ref[2 * g + a, s, rows, :] = ref[c, :, lanes]
    for s in range(slabs):
        outs = [scr_ref[2 * g, s] for g in range(n)]
        lses = [scr_ref[2 * g + 1, s] for g in range(n)]
        m = functools.reduce(jnp.maximum, lses)
        ws = [jnp.exp(l - m) for l in lses]
        den = functools.reduce(lambda a, b: a + b, ws)
        num = functools.reduce(lambda a, b: a + b, [w * o for w, o in zip(ws, outs)])
        out_ref[:, s * V7X_LANES:(s + 1) * V7X_LANES] = (num / den).astype(out_ref.dtype)


def combine_groups(outs, lses):
    batch, _, _, width = outs[0].shape
    dilations = tuple(o.shape[1] for o in outs)
    seq = outs[0].shape[1] * outs[0].shape[2]
    bm = _pick(seq, COMBINE_ROWS)
    bn = _pick(width, COMBINE_COLS)
    pos_tiles = seq // bm

    def spec(d):
        return pl.BlockSpec((None, d, bm // d, bn), lambda i, j: (i // pos_tiles, 0, i % pos_tiles, j))

    specs = [spec(d) for d in dilations]
    return pl.pallas_call(
        functools.partial(_combine_kernel, dilations=dilations),
        grid=(batch * pos_tiles, width // bn),
        in_specs=specs + specs,
        out_specs=pl.BlockSpec((bm, bn), lambda i, j: (i, j)),
        out_shape=jax.ShapeDtypeStruct((batch * seq, width), BF16),
        scratch_shapes=[pltpu.VMEM((2 * len(outs), bn // V7X_LANES, bm, V7X_LANES), F32)],
        compiler_params=_params("parallel", "arbitrary"),
        name="combine_groups",
    )(*outs, *lses)


CAST_BLOCK_BYTES = 4 * 1024 * 1024


def _cast_kernel(w_ref, o_ref):
    o_ref[...] = w_ref[...].astype(o_ref.dtype)


def cast_layer(w, layer):
    _, r, c = w.shape
    br = _pick(r, max(16, CAST_BLOCK_BYTES // (4 * c)))
    return pl.pallas_call(
        _cast_kernel,
        grid=(r // br,),
        in_specs=[pl.BlockSpec((None, br, c), lambda i: (layer, i, 0))],
        out_specs=pl.BlockSpec((br, c), lambda i: (i, 0)),
        out_shape=jax.ShapeDtypeStruct((r, c), BF16),
        compiler_params=_params("parallel"),
        name="cast_layer",
    )(w)


def _rope_tables(seq):
    inv = ROPE_THETA ** (-jnp.arange(0, HEAD_DIM, 2, dtype=F32) / HEAD_DIM)
    ang = jnp.arange(seq, dtype=F32)[:, None] * inv[None, :]
    cos, sin = jnp.cos(ang), jnp.sin(ang)
    return jnp.concatenate([cos, cos], axis=-1), jnp.concatenate([-sin, sin], axis=-1)


def kernel(x, ffn1_norm, ffn1_wg, ffn1_wu, ffn1_wd, mix_norm, ffn2_norm, ffn2_wg, ffn2_wu, ffn2_wd, conv_pw1, conv_pw1_b, conv_dw, conv_dw_b, conv_ln_g, conv_ln_b, conv_pw2, conv_pw2_b, attn_wqkv, attn_wo, final_norm):
    batch, seq, d = x.shape
    depth = ffn1_norm.shape[0]
    n_mixers = 2
    width = attn_wo.shape[1]
    cos_full, sin_signed = _rope_tables(seq)

    xt = x.reshape(batch * seq, d)
    for i in range(depth):
        xt = ffn(xt, ffn1_norm[i], cast_layer(ffn1_wg, i), cast_layer(ffn1_wu, i),
                 cast_layer(ffn1_wd, i))
        h = rmsnorm(xt, mix_norm[i], BF16)
        j = i // n_mixers
        if i % n_mixers == 0:
            glu = glu_proj(h, cast_layer(conv_pw1, j), conv_pw1_b[j])
            act = conv_core(glu.reshape(batch, seq, d), conv_dw[j], conv_dw_b[j],
                            conv_ln_g[j], conv_ln_b[j])
            xt = res_proj(act.reshape(batch * seq, d), cast_layer(conv_pw2, j), conv_pw2_b[j], xt)
        else:
            wqkv = cast_layer(attn_wqkv, j)
            outs, lses = [], []
            for g, (window, dilation) in enumerate(ATTN_GROUPS):
                qkv = qkv_proj(h, wqkv, g, dilation, cos_full, sin_signed, batch, seq, width)
                o, l = band_attention(qkv, dilation, window // (2 * dilation))
                outs.append(o)
                lses.append(l)
            mixed = combine_groups(outs, lses)
            xt = res_proj(mixed, cast_layer(attn_wo, j), jnp.zeros((d,), F32), xt)
        xt = ffn(xt, ffn2_norm[i], cast_layer(ffn2_wg, i), cast_layer(ffn2_wu, i),
                 cast_layer(ffn2_wd, i))
    return rmsnorm(xt, final_norm, F32).reshape(batch, seq, d)
```

```python
import functools

import jax
import jax.numpy as jnp
from jax import lax
from jax.experimental import pallas as pl
from jax.experimental.pallas import tpu as pltpu

F32 = jnp.float32
BF16 = jnp.bfloat16

HEAD_DIM = 128
ATTN_GROUPS = ((128, 1), (512, 4), (2048, 16))
CONV_WIDTH = 31
ROPE_THETA = 10000.0
RMS_EPS = 1e-6
LN_EPS = 1e-5
NEG_INF = -1e30

V7X_LANES = 128
V7X_SUBLANES = 8
V7X_VMEM_LIMIT_BYTES = 56 * 1024 * 1024


def _params(*semantics):
    return pltpu.CompilerParams(
        dimension_semantics=semantics, vmem_limit_bytes=V7X_VMEM_LIMIT_BYTES)


def _pick(n, pref):
    if n <= pref:
        return n
    t = 1 << (pref.bit_length() - 1)
    while n % t:
        t //= 2
    return t


def _rms_rows(x, gain):
    ms = jnp.mean(x * x, axis=-1, keepdims=True)
    return x * lax.rsqrt(ms + RMS_EPS) * gain


def _rmsnorm_kernel(x_ref, g_ref, o_ref):
    o_ref[...] = _rms_rows(x_ref[...], g_ref[...]).astype(o_ref.dtype)


def rmsnorm(x, gain, out_dtype):
    t, d = x.shape
    bm = _pick(t, 256)
    return pl.pallas_call(
        _rmsnorm_kernel,
        grid=(t // bm,),
        in_specs=[pl.BlockSpec((bm, d), lambda i: (i, 0)),
                  pl.BlockSpec((1, d), lambda i: (0, 0))],
        out_specs=pl.BlockSpec((bm, d), lambda i: (i, 0)),
        out_shape=jax.ShapeDtypeStruct((t, d), out_dtype),
        compiler_params=_params("parallel"),
        name="rmsnorm",
    )(x, gain.reshape(1, d))


FFN_LOAD_ROWS = 128
FFN_DOWN_COLS = 512


def _x_chunk_copy(x_hbm, o_ref, sems, tile, chunk):
    rows = o_ref.shape[0]
    src = x_hbm.at[pl.ds(tile * rows + chunk * FFN_LOAD_ROWS, FFN_LOAD_ROWS)]
    return pltpu.make_async_copy(src, o_ref.at[pl.ds(chunk * FFN_LOAD_ROWS, FFN_LOAD_ROWS)],
                                 sems.at[chunk])


def _ffn_kernel(*refs, n_side):
    x_hbm, g_ref, wg_ref, wu_ref, wd_ref = refs[:5]
    side_in = refs[5:5 + n_side]
    o_ref = refs[5 + n_side]
    side_out = refs[6 + n_side:6 + 2 * n_side]
    h_ref, sems = refs[6 + 2 * n_side:]
    bm, d = o_ref.shape

    @pl.when(pl.program_id(1) == 0)
    def _():
        n_chunks = bm // FFN_LOAD_ROWS
        for c in range(n_chunks):
            _x_chunk_copy(x_hbm, o_ref, sems, pl.program_id(0), c).start()
        for c in range(n_chunks):
            _x_chunk_copy(x_hbm, o_ref, sems, pl.program_id(0), c).wait()
            rows = pl.ds(c * FFN_LOAD_ROWS, FFN_LOAD_ROWS)
            h_ref[rows, :] = _rms_rows(o_ref[rows, :], g_ref[...]).astype(BF16)

    h = h_ref[...]
    gate = jnp.dot(h, wg_ref[...], preferred_element_type=F32)
    up = jnp.dot(h, wu_ref[...], preferred_element_type=F32)
    act = (gate * jax.nn.sigmoid(gate) * up * 0.5).astype(BF16)
    step = min(d, FFN_DOWN_COLS)
    for c0 in range(0, d, step):
        cols = slice(c0, c0 + step)
        o_ref[:, cols] += jnp.dot(act, wd_ref[:, cols], preferred_element_type=F32)

    for src, dst in zip(side_in, side_out):
        dst[...] = src[...].astype(dst.dtype)


def _side_cast_specs(w, layer, n_steps, step_of):
    _, r, c = w.shape
    rb = 16
    while r // rb > n_steps:
        rb *= 2
    assert r % rb == 0
    last = r // rb - 1
    in_spec = pl.BlockSpec((None, rb, c), lambda *g: (layer, jnp.minimum(step_of(*g), last), 0))
    out_spec = pl.BlockSpec((rb, c), lambda *g: (jnp.minimum(step_of(*g), last), 0))
    return in_spec, out_spec, jax.ShapeDtypeStruct((r, c), BF16)


def ffn(x, gain, wg, wu, wd, side=()):
    t, d = x.shape
    f = wg.shape[1]
    bm = _pick(t, 1024)
    bf = _pick(f, 256)
    assert bm % FFN_LOAD_ROWS == 0 and d % min(d, FFN_DOWN_COLS) == 0
    grid = (t // bm, f // bf)
    side_specs = [_side_cast_specs(w, layer, grid[0] * grid[1], lambda i, j: i * grid[1] + j)
                  for w, layer in side]
    outs = pl.pallas_call(
        functools.partial(_ffn_kernel, n_side=len(side)),
        grid=grid,
        in_specs=[pl.BlockSpec(memory_space=pl.ANY),
                  pl.BlockSpec((1, d), lambda i, j: (0, 0)),
                  pl.BlockSpec((d, bf), lambda i, j: (0, j)),
                  pl.BlockSpec((d, bf), lambda i, j: (0, j)),
                  pl.BlockSpec((bf, d), lambda i, j: (j, 0))] + [s[0] for s in side_specs],
        out_specs=[pl.BlockSpec((bm, d), lambda i, j: (i, 0), pipeline_mode=pl.Buffered(1))]
        + [s[1] for s in side_specs],
        out_shape=[jax.ShapeDtypeStruct((t, d), F32)] + [s[2] for s in side_specs],
        scratch_shapes=[pltpu.VMEM((bm, d), BF16),
                        pltpu.SemaphoreType.DMA((bm // FFN_LOAD_ROWS,))],
        compiler_params=_params("arbitrary", "arbitrary"),
        name="ffn",
    )(x, gain.reshape(1, d), wg, wu, wd, *[w for w, _ in side])
    return outs[0], list(outs[1:])


def _glu_kernel(h_ref, wa_ref, wb_ref, ba_ref, bb_ref, o_ref):
    h = h_ref[...]
    a = jnp.dot(h, wa_ref[...], preferred_element_type=F32) + ba_ref[...]
    b = jnp.dot(h, wb_ref[...], preferred_element_type=F32) + bb_ref[...]
    o_ref[...] = a * jax.nn.sigmoid(b)


def glu_proj(h, w, bias):
    t, d = h.shape
    n = w.shape[1] // 2
    bm = _pick(t, 1024)
    bn = _pick(n, 512)
    nb = n // bn
    bias = bias.reshape(1, 2 * n)
    return pl.pallas_call(
        _glu_kernel,
        grid=(t // bm, nb),
        in_specs=[pl.BlockSpec((bm, d), lambda i, j: (i, 0)),
                  pl.BlockSpec((d, bn), lambda i, j: (0, j)),
                  pl.BlockSpec((d, bn), lambda i, j: (0, j + nb)),
                  pl.BlockSpec((1, bn), lambda i, j: (0, j)),
                  pl.BlockSpec((1, bn), lambda i, j: (0, j + nb))],
        out_specs=pl.BlockSpec((bm, bn), lambda i, j: (i, j)),
        out_shape=jax.ShapeDtypeStruct((t, n), F32),
        compiler_params=_params("parallel", "arbitrary"),
        name="glu_proj",
    )(h, w, w, bias, bias)


def _res_proj_kernel(h_ref, w_ref, b_ref, x_ref, o_ref):
    acc = jnp.dot(h_ref[...], w_ref[...], preferred_element_type=F32)
    o_ref[...] = x_ref[...] + (acc + b_ref[...])


def res_proj(h, w, bias, x):
    t, k = h.shape
    n = w.shape[1]
    bm = _pick(t, 1024)
    bn = _pick(n, 512)
    return pl.pallas_call(
        _res_proj_kernel,
        grid=(t // bm, n // bn),
        in_specs=[pl.BlockSpec((bm, k), lambda i, j: (i, 0)),
                  pl.BlockSpec((k, bn), lambda i, j: (0, j)),
                  pl.BlockSpec((1, bn), lambda i, j: (0, j)),
                  pl.BlockSpec((bm, bn), lambda i, j: (i, j))],
        out_specs=pl.BlockSpec((bm, bn), lambda i, j: (i, j)),
        out_shape=jax.ShapeDtypeStruct((t, n), F32),
        compiler_params=_params("parallel", "arbitrary"),
        name="res_proj",
    )(h, w, bias.reshape(1, n), x)


def _qkv_kernel(*refs, dilation, tiles_per_part, n_side):
    h_ref, w_ref, cos_ref, sin_ref = refs[:4]
    side_in = refs[4:4 + n_side]
    o_ref = refs[4 + n_side]
    side_out = refs[5 + n_side:5 + 2 * n_side]
    scratch = refs[5 + 2 * n_side:]
    for src, dst in zip(side_in, side_out):
        dst[...] = src[...].astype(dst.dtype)
    acc = jnp.dot(h_ref[...], w_ref[...], preferred_element_type=F32)
    bm, bn = acc.shape
    rotate = pl.program_id(1) // tiles_per_part < 2
    cos = jnp.where(rotate, cos_ref[...], 1.0)
    sin = jnp.where(rotate, sin_ref[...], 0.0)
    for hh in range(bn // HEAD_DIM):
        cols = slice(hh * HEAD_DIM, (hh + 1) * HEAD_DIM)
        t = acc[:, cols]
        val = t * cos + pltpu.roll(t, HEAD_DIM // 2, axis=1) * sin
        if dilation == 1:
            o_ref[0, :, cols] = val.astype(o_ref.dtype)
        else:
            scratch[0][hh] = val
    if dilation > 1:
        rows = bm // dilation
        for c in range(dilation):
            for hh in range(bn // HEAD_DIM):
                cols = slice(hh * HEAD_DIM, (hh + 1) * HEAD_DIM)
                o_ref[c, :, cols] = scratch[0][hh, pl.ds(c, rows, stride=dilation), :].astype(o_ref.dtype)


def qkv_proj(h, w, group, dilation, cos_full, sin_signed, batch, seq, part_width, side=()):
    t, d = h.shape
    n = 3 * part_width
    bm = _pick(seq, 1024)
    bn = _pick(part_width, 512)
    pos_tiles = seq // bm
    col_tiles = n // bn
    length = seq // dilation
    kern = functools.partial(_qkv_kernel, dilation=dilation, tiles_per_part=part_width // bn,
                             n_side=len(side))
    scratch = [] if dilation == 1 else [pltpu.VMEM((bn // HEAD_DIM, bm, HEAD_DIM), F32)]
    grid = (t // bm, col_tiles)
    side_specs = [_side_cast_specs(sw, layer, grid[0] * grid[1], lambda i, j: i * col_tiles + j)
                  for sw, layer in side]
    outs = pl.pallas_call(
        kern,
        grid=grid,
        in_specs=[pl.BlockSpec((bm, d), lambda i, j: (i, 0)),
                  pl.BlockSpec((d, bn), lambda i, j: (0, group * col_tiles + j)),
                  pl.BlockSpec((bm, HEAD_DIM), lambda i, j: (i % pos_tiles, 0)),
                  pl.BlockSpec((bm, HEAD_DIM), lambda i, j: (i % pos_tiles, 0))]
        + [s[0] for s in side_specs],
        out_specs=[pl.BlockSpec((None, dilation, bm // dilation, bn),
                                lambda i, j: (i // pos_tiles, 0, i % pos_tiles, j))]
        + [s[1] for s in side_specs],
        out_shape=[jax.ShapeDtypeStruct((batch, dilation, length, n), BF16)]
        + [s[2] for s in side_specs],
        scratch_shapes=scratch,
        compiler_params=_params("arbitrary", "arbitrary"),
        name=f"qkv_proj_d{dilation}",
    )(h, w, cos_full, sin_signed, *[sw for sw, _ in side])
    return outs[0], list(outs[1:])


CONV_HALO = 16
CONV_ROWS = 64
CONV_COLS = 256


def _conv_kernel(prev_ref, cur_ref, next_ref, dw_ref, dwb_ref, lng_ref, lnb_ref,
                 o_ref, xs_ref, cv_ref, sh_ref):
    ts, d = cur_ref.shape[1], cur_ref.shape[2]
    i = pl.program_id(1)
    last = pl.num_programs(1) - 1
    xs_ref[0:CONV_HALO, :] = jnp.where(i > 0, prev_ref[0], 0.0)
    xs_ref[CONV_HALO:CONV_HALO + ts, :] = cur_ref[0]
    xs_ref[CONV_HALO + ts:, :] = jnp.where(i < last, next_ref[0], 0.0)

    n_rows = ts // CONV_ROWS
    n_cols = d // CONV_COLS
    shift = CONV_HALO - CONV_WIDTH // 2

    span = sh_ref.shape[1]

    def chunk(idx, carry):
        cols = pl.ds(pl.multiple_of(idx * CONV_COLS, CONV_COLS), CONV_COLS)
        for b in range(V7X_SUBLANES):
            sh_ref[b] = xs_ref[b:b + span, cols]
        for r0 in range(0, ts, CONV_ROWS):
            acc = jnp.zeros((CONV_ROWS, CONV_COLS), F32)
            for k in range(CONV_WIDTH):
                a, b = divmod(k + shift, V7X_SUBLANES)
                row = r0 + a * V7X_SUBLANES
                acc = acc + sh_ref[b, row:row + CONV_ROWS, :] * dw_ref[k:k + 1, cols]
            cv_ref[r0:r0 + CONV_ROWS, cols] = acc + dwb_ref[:, cols]
        return carry

    lax.fori_loop(0, n_cols, chunk, 0)

    def norm(r, carry):
        r0 = pl.multiple_of(r * CONV_ROWS, CONV_ROWS)
        hf = cv_ref[pl.ds(r0, CONV_ROWS), :]
        mu = jnp.mean(hf, axis=-1, keepdims=True)
        cen = hf - mu
        var = jnp.mean(cen * cen, axis=-1, keepdims=True)
        y = cen * lax.rsqrt(var + LN_EPS) * lng_ref[...] + lnb_ref[...]
        o_ref[0, pl.ds(r0, CONV_ROWS), :] = (y * jax.nn.sigmoid(y)).astype(o_ref.dtype)
        return carry

    lax.fori_loop(0, n_rows, norm, 0)


def conv_core(x, dw, dw_b, ln_g, ln_b):
    b, s, d = x.shape
    ts = _pick(s, 256)
    hb = ts // CONV_HALO
    n_halo_blocks = s // CONV_HALO
    span_extra = (CONV_WIDTH - 1 + CONV_HALO - CONV_WIDTH // 2) // V7X_SUBLANES * V7X_SUBLANES
    assert span_extra + V7X_SUBLANES <= 2 * CONV_HALO
    row = lambda v: v.reshape(1, d)
    return pl.pallas_call(
        _conv_kernel,
        grid=(b, s // ts),
        in_specs=[
            pl.BlockSpec((1, CONV_HALO, d), lambda bi, i: (bi, jnp.maximum(i * hb - 1, 0), 0)),
            pl.BlockSpec((1, ts, d), lambda bi, i: (bi, i, 0)),
            pl.BlockSpec((1, CONV_HALO, d),
                         lambda bi, i: (bi, jnp.minimum((i + 1) * hb, n_halo_blocks - 1), 0)),
            pl.BlockSpec((CONV_WIDTH, d), lambda bi, i: (0, 0)),
            pl.BlockSpec((1, d), lambda bi, i: (0, 0)),
            pl.BlockSpec((1, d), lambda bi, i: (0, 0)),
            pl.BlockSpec((1, d), lambda bi, i: (0, 0)),
        ],
        out_specs=pl.BlockSpec((1, ts, d), lambda bi, i: (bi, i, 0)),
        out_shape=jax.ShapeDtypeStruct((b, s, d), BF16),
        scratch_shapes=[pltpu.VMEM((ts + 2 * CONV_HALO, d), F32),
                        pltpu.VMEM((ts, d), F32),
                        pltpu.VMEM((V7X_SUBLANES, ts + span_extra, CONV_COLS), F32)],
        compiler_params=_params("parallel", "arbitrary"),
        name="conv_core",
    )(x, x, x, dw, row(dw_b), row(ln_g), row(ln_b))


ATTN_HEADS_PER_STEP = 8
ATTN_SUB_ROWS = 128


def _band_attn_kernel(q_ref, kp_ref, kc_ref, kn_ref, vp_ref, vc_ref, vn_ref,
                      o_ref, lse_ref, *, radius, length):
    bq = q_ref.shape[0]
    sub = min(bq, ATTN_SUB_ROWS)
    nk = sub + 2 * radius
    j0 = pl.program_id(3) * bq
    qi = lax.broadcasted_iota(jnp.int32, (sub, nk), 0)
    kj = lax.broadcasted_iota(jnp.int32, (sub, nk), 1) - radius
    band = jnp.abs(kj - qi) <= radius
    scale = HEAD_DIM ** -0.5
    for r0 in range(0, bq, sub):
        kabs = kj + (j0 + r0)
        valid = band & (kabs >= 0) & (kabs < length)
        for hh in range(q_ref.shape[1] // HEAD_DIM):
            cols = slice(hh * HEAD_DIM, (hh + 1) * HEAD_DIM)
            q = q_ref[r0:r0 + sub, cols]
            parts = []
            for src, lo in ((kp_ref, vp_ref), 0), ((kc_ref, vc_ref), radius), ((kn_ref, vn_ref), radius + bq):
                a, b = max(r0, lo), min(r0 + nk, lo + src[0].shape[0])
                if a < b:
                    parts.append((src, a - lo, b - lo))
            keys = jnp.concatenate([s[0][a:b, cols] for s, a, b in parts], axis=0)
            vals = jnp.concatenate([s[1][a:b, cols] for s, a, b in parts], axis=0)
            sc = lax.dot_general(q, keys, (((1,), (1,)), ((), ())),
                                 preferred_element_type=F32) * scale
            sc = jnp.where(valid, sc, NEG_INF)
            m = jnp.max(sc, axis=-1, keepdims=True)
            p = jnp.exp(sc - m)
            den = jnp.sum(p, axis=-1, keepdims=True)
            o = jnp.dot(p.astype(BF16), vals, preferred_element_type=F32) / den
            o_ref[r0:r0 + sub, cols] = o
            lse_ref[r0:r0 + sub, cols] = jnp.broadcast_to(m + jnp.log(den), (sub, HEAD_DIM))


def band_attention(qkv, dilation, radius):
    batch, _, length, total = qkv.shape
    width = total // 3
    w = min(ATTN_HEADS_PER_STEP * HEAD_DIM, width)
    bq = _pick(length, 256)
    assert bq % radius == 0 and length % radius == 0
    rb = bq // radius
    n_rblocks = length // radius
    hblocks = width // w

    def cur(part):
        return pl.BlockSpec((None, None, bq, w), lambda b, c, h, i: (b, c, i, part * hblocks + h))

    def prev(part):
        return pl.BlockSpec((None, None, radius, w),
                            lambda b, c, h, i: (b, c, jnp.maximum(i * rb - 1, 0), part * hblocks + h))

    def nxt(part):
        return pl.BlockSpec(
            (None, None, radius, w),
            lambda b, c, h, i: (b, c, jnp.minimum((i + 1) * rb, n_rblocks - 1), part * hblocks + h))

    out_spec = pl.BlockSpec((None, None, bq, w), lambda b, c, h, i: (b, c, i, h))
    kern = functools.partial(_band_attn_kernel, radius=radius, length=length)
    return pl.pallas_call(
        kern,
        grid=(batch, dilation, hblocks, length // bq),
        in_specs=[cur(0), prev(1), cur(1), nxt(1), prev(2), cur(2), nxt(2)],
        out_specs=[out_spec, out_spec],
        out_shape=[jax.ShapeDtypeStruct((batch, dilation, length, width), F32)] * 2,
        compiler_params=_params("parallel", "parallel", "parallel", "arbitrary"),
        name=f"band_attn_d{dilation}",
    )(qkv, qkv, qkv, qkv, qkv, qkv, qkv)


COMBINE_ROWS = 256
COMBINE_COLS = 1024


def _combine_kernel(*refs, dilations):
    n = len(dilations)
    o_refs, l_refs, out_ref, scr_ref = refs[:n], refs[n:2 * n], refs[2 * n], refs[2 * n + 1]
    bm, bn = out_ref.shape
    slabs = bn // V7X_LANES
    for g, d in enumerate(dilations):
        for a, ref in enumerate((o_refs[g], l_refs[g])):
            for c in range(d):
                for s in range(slabs):
                    lanes = slice(s * V7X_LANES, (s + 1) * V7X_LANES)
                    rows = pl.ds(c, bm // d, stride=d) if d > 1 else slice(None)
                    scr_ref[2 * g + a, s, rows, :] = ref[c, :, lanes]
    for s in range(slabs):
        outs = [scr_ref[2 * g, s] for g in range(n)]
        lses = [scr_ref[2 * g + 1, s] for g in range(n)]
        m = functools.reduce(jnp.maximum, lses)
        ws = [jnp.exp(l - m) for l in lses]
        den = functools.reduce(lambda a, b: a + b, ws)
        num = functools.reduce(lambda a, b: a + b, [w * o for w, o in zip(ws, outs)])
        out_ref[:, s * V7X_LANES:(s + 1) * V7X_LANES] = (num / den).astype(out_ref.dtype)


def combine_groups(outs, lses):
    batch, _, _, width = outs[0].shape
    dilations = tuple(o.shape[1] for o in outs)
    seq = outs[0].shape[1] * outs[0].shape[2]
    bm = _pick(seq, COMBINE_ROWS)
    bn = _pick(width, COMBINE_COLS)
    pos_tiles = seq // bm

    def spec(d):
        return pl.BlockSpec((None, d, bm // d, bn), lambda i, j: (i // pos_tiles, 0, i % pos_tiles, j))

    specs = [spec(d) for d in dilations]
    return pl.pallas_call(
        functools.partial(_combine_kernel, dilations=dilations),
        grid=(batch * pos_tiles, width // bn),
        in_specs=specs + specs,
        out_specs=pl.BlockSpec((bm, bn), lambda i, j: (i, j)),
        out_shape=jax.ShapeDtypeStruct((batch * seq, width), BF16),
        scratch_shapes=[pltpu.VMEM((2 * len(outs), bn // V7X_LANES, bm, V7X_LANES), F32)],
        compiler_params=_params("parallel", "arbitrary"),
        name="combine_groups",
    )(*outs, *lses)


CAST_BLOCK_BYTES = 4 * 1024 * 1024


def _cast_kernel(w_ref, o_ref):
    o_ref[...] = w_ref[...].astype(o_ref.dtype)


def cast_layer(w, layer):
    _, r, c = w.shape
    br = _pick(r, max(16, CAST_BLOCK_BYTES // (4 * c)))
    return pl.pallas_call(
        _cast_kernel,
        grid=(r // br,),
        in_specs=[pl.BlockSpec((None, br, c), lambda i: (layer, i, 0))],
        out_specs=pl.BlockSpec((br, c), lambda i: (i, 0)),
        out_shape=jax.ShapeDtypeStruct((r, c), BF16),
        compiler_params=_params("parallel"),
        name="cast_layer",
    )(w)


def _rope_tables(seq):
    inv = ROPE_THETA ** (-jnp.arange(0, HEAD_DIM, 2, dtype=F32) / HEAD_DIM)
    ang = jnp.arange(seq, dtype=F32)[:, None] * inv[None, :]
    cos, sin = jnp.cos(ang), jnp.sin(ang)
    return jnp.concatenate([cos, cos], axis=-1), jnp.concatenate([-sin, sin], axis=-1)


def kernel(x, ffn1_norm, ffn1_wg, ffn1_wu, ffn1_wd, mix_norm, ffn2_norm, ffn2_wg, ffn2_wu, ffn2_wd, conv_pw1, conv_pw1_b, conv_dw, conv_dw_b, conv_ln_g, conv_ln_b, conv_pw2, conv_pw2_b, attn_wqkv, attn_wo, final_norm):
    batch, seq, d = x.shape
    assert ffn1_norm.shape[0] == 2 and conv_pw1.shape[0] == 1 and attn_wqkv.shape[0] == 1
    width = attn_wo.shape[1]
    cos_full, sin_signed = _rope_tables(seq)
    xt = x.reshape(batch * seq, d)

    w_ffn = [cast_layer(w, 0) for w in (ffn1_wg, ffn1_wu, ffn1_wd)]
    xt, (pw1, pw2, *w_ffn) = ffn(
        xt, ffn1_norm[0], *w_ffn,
        side=[(conv_pw1, 0), (conv_pw2, 0), (ffn2_wg, 0), (ffn2_wu, 0), (ffn2_wd, 0)])
    h = rmsnorm(xt, mix_norm[0], BF16)
    glu = glu_proj(h, pw1, conv_pw1_b[0])
    act = conv_core(glu.reshape(batch, seq, d), conv_dw[0], conv_dw_b[0], conv_ln_g[0], conv_ln_b[0])
    xt = res_proj(act.reshape(batch * seq, d), pw2, conv_pw2_b[0], xt)
    xt, w_ffn = ffn(xt, ffn2_norm[0], *w_ffn, side=[(ffn1_wg, 1), (ffn1_wu, 1), (ffn1_wd, 1)])

    xt, (wqkv, wo) = ffn(xt, ffn1_norm[1], *w_ffn, side=[(attn_wqkv, 0), (attn_wo, 0)])
    h = rmsnorm(xt, mix_norm[1], BF16)
    outs, lses, w_ffn = [], [], []
    for g, ((window, dilation), w_next) in enumerate(zip(ATTN_GROUPS, (ffn2_wg, ffn2_wu, ffn2_wd))):
        qkv, cast = qkv_proj(h, wqkv, g, dilation, cos_full, sin_signed, batch, seq, width,
                             side=[(w_next, 1)])
        w_ffn += cast
        o, l = band_attention(qkv, dilation, window // (2 * dilation))
        outs.append(o)
        lses.append(l)
    mixed = combine_groups(outs, lses)
    xt = res_proj(mixed, wo, jnp.zeros((d,), F32), xt)
    xt, _ = ffn(xt, ffn2_norm[1], *w_ffn)
    return rmsnorm(xt, final_norm, F32).reshape(batch, seq, d)
```

```python
import functools

import jax
import jax.numpy as jnp
from jax import lax
from jax.experimental import pallas as pl
from jax.experimental.pallas import tpu as pltpu

F32 = jnp.float32
BF16 = jnp.bfloat16

HEAD_DIM = 128
ATTN_GROUPS = ((128, 1), (512, 4), (2048, 16))
CONV_WIDTH = 31
ROPE_THETA = 10000.0
RMS_EPS = 1e-6
LN_EPS = 1e-5
NEG_INF = -1e30

V7X_LANES = 128
V7X_SUBLANES = 8
V7X_VMEM_LIMIT_BYTES = 56 * 1024 * 1024


def _params(*semantics):
    return pltpu.CompilerParams(
        dimension_semantics=semantics, vmem_limit_bytes=V7X_VMEM_LIMIT_BYTES)


def _pick(n, pref):
    if n <= pref:
        return n
    t = 1 << (pref.bit_length() - 1)
    while n % t:
        t //= 2
    return t


def _rms_rows(x, gain):
    ms = jnp.mean(x * x, axis=-1, keepdims=True)
    return x * lax.rsqrt(ms + RMS_EPS) * gain


def _rmsnorm_kernel(x_ref, g_ref, o_ref):
    o_ref[...] = _rms_rows(x_ref[...], g_ref[...]).astype(o_ref.dtype)


def rmsnorm(x, gain, out_dtype):
    t, d = x.shape
    bm = _pick(t, 256)
    return pl.pallas_call(
        _rmsnorm_kernel,
        grid=(t // bm,),
        in_specs=[pl.BlockSpec((bm, d), lambda i: (i, 0)),
                  pl.BlockSpec((1, d), lambda i: (0, 0))],
        out_specs=pl.BlockSpec((bm, d), lambda i: (i, 0)),
        out_shape=jax.ShapeDtypeStruct((t, d), out_dtype),
        compiler_params=_params("parallel"),
        name="rmsnorm",
    )(x, gain.reshape(1, d))


def _rmsnorm_classes_kernel(x_ref, g_ref, *refs, dilations):
    outs, scr_ref = refs[:-1], refs[-1]
    y = _rms_rows(x_ref[...], g_ref[...])
    bm, d = y.shape
    slabs = d // V7X_LANES
    if any(dil > 1 for dil in dilations):
        for s in range(slabs):
            scr_ref[s] = y[:, s * V7X_LANES:(s + 1) * V7X_LANES]
    for o_ref, dil in zip(outs, dilations):
        if dil == 1:
            o_ref[0] = y.astype(o_ref.dtype)
            continue
        for c in range(dil):
            for s in range(slabs):
                o_ref[c, :, s * V7X_LANES:(s + 1) * V7X_LANES] = (
                    scr_ref[s, pl.ds(c, bm // dil, stride=dil), :].astype(o_ref.dtype))


def rmsnorm_classes(x, gain, batch, seq, dilations):
    t, d = x.shape
    bm = _pick(seq, 256)
    pos_tiles = seq // bm
    out_specs = [pl.BlockSpec((None, dil, bm // dil, d), lambda i: (i // pos_tiles, 0, i % pos_tiles, 0))
                 for dil in dilations]
    outs = pl.pallas_call(
        functools.partial(_rmsnorm_classes_kernel, dilations=tuple(dilations)),
        grid=(t // bm,),
        in_specs=[pl.BlockSpec((bm, d), lambda i: (i, 0)),
                  pl.BlockSpec((1, d), lambda i: (0, 0))],
        out_specs=out_specs,
        out_shape=[jax.ShapeDtypeStruct((batch, dil, seq // dil, d), BF16) for dil in dilations],
        scratch_shapes=[pltpu.VMEM((d // V7X_LANES, bm, V7X_LANES), F32)],
        compiler_params=_params("parallel"),
        name="rmsnorm_classes",
    )(x, gain.reshape(1, d))
    return [o.reshape(t, d) for o in outs]


FFN_LOAD_ROWS = 128
FFN_DOWN_COLS = 512


def _x_chunk_copy(x_hbm, o_ref, sems, tile, chunk):
    rows = o_ref.shape[0]
    src = x_hbm.at[pl.ds(tile * rows + chunk * FFN_LOAD_ROWS, FFN_LOAD_ROWS)]
    return pltpu.make_async_copy(src, o_ref.at[pl.ds(chunk * FFN_LOAD_ROWS, FFN_LOAD_ROWS)],
                                 sems.at[chunk])


def _ffn_kernel(*refs, n_side):
    x_hbm, g_ref, wg_ref, wu_ref, wd_ref = refs[:5]
    side_in = refs[5:5 + n_side]
    o_ref = refs[5 + n_side]
    side_out = refs[6 + n_side:6 + 2 * n_side]
    h_ref, sems = refs[6 + 2 * n_side:]
    bm, d = o_ref.shape

    @pl.when(pl.program_id(1) == 0)
    def _():
        n_chunks = bm // FFN_LOAD_ROWS
        for c in range(n_chunks):
            _x_chunk_copy(x_hbm, o_ref, sems, pl.program_id(0), c).start()
        for c in range(n_chunks):
            _x_chunk_copy(x_hbm, o_ref, sems, pl.program_id(0), c).wait()
            rows = pl.ds(c * FFN_LOAD_ROWS, FFN_LOAD_ROWS)
            h_ref[rows, :] = _rms_rows(o_ref[rows, :], g_ref[...]).astype(BF16)

    h = h_ref[...]
    gate = jnp.dot(h, wg_ref[...], preferred_element_type=F32)
    up = jnp.dot(h, wu_ref[...], preferred_element_type=F32)
    act = (gate * jax.nn.sigmoid(gate) * up * 0.5).astype(BF16)
    step = min(d, FFN_DOWN_COLS)
    for c0 in range(0, d, step):
        cols = slice(c0, c0 + step)
        o_ref[:, cols] += jnp.dot(act, wd_ref[:, cols], preferred_element_type=F32)

    for src, dst in zip(side_in, side_out):
        dst[...] = src[...].astype(dst.dtype)


def _side_cast_specs(w, layer, n_steps, step_of):
    _, r, c = w.shape
    rb = 16
    while r // rb > n_steps:
        rb *= 2
    assert r % rb == 0
    last = r // rb - 1
    in_spec = pl.BlockSpec((None, rb, c), lambda *g: (layer, jnp.minimum(step_of(*g), last), 0))
    out_spec = pl.BlockSpec((rb, c), lambda *g: (jnp.minimum(step_of(*g), last), 0))
    return in_spec, out_spec, jax.ShapeDtypeStruct((r, c), BF16)


def ffn(x, gain, wg, wu, wd, side=()):
    t, d = x.shape
    f = wg.shape[1]
    bm = _pick(t, 1024)
    bf = _pick(f, 256)
    assert bm % FFN_LOAD_ROWS == 0 and d % min(d, FFN_DOWN_COLS) == 0
    grid = (t // bm, f // bf)
    side_specs = [_side_cast_specs(w, layer, grid[0] * grid[1], lambda i, j: i * grid[1] + j)
                  for w, layer in side]
    outs = pl.pallas_call(
        functools.partial(_ffn_kernel, n_side=len(side)),
        grid=grid,
        in_specs=[pl.BlockSpec(memory_space=pl.ANY),
                  pl.BlockSpec((1, d), lambda i, j: (0, 0)),
                  pl.BlockSpec((d, bf), lambda i, j: (0, j)),
                  pl.BlockSpec((d, bf), lambda i, j: (0, j)),
                  pl.BlockSpec((bf, d), lambda i, j: (j, 0))] + [s[0] for s in side_specs],
        out_specs=[pl.BlockSpec((bm, d), lambda i, j: (i, 0), pipeline_mode=pl.Buffered(1))]
        + [s[1] for s in side_specs],
        out_shape=[jax.ShapeDtypeStruct((t, d), F32)] + [s[2] for s in side_specs],
        scratch_shapes=[pltpu.VMEM((bm, d), BF16),
                        pltpu.SemaphoreType.DMA((bm // FFN_LOAD_ROWS,))],
        compiler_params=_params("arbitrary", "arbitrary"),
        name="ffn",
    )(x, gain.reshape(1, d), wg, wu, wd, *[w for w, _ in side])
    return outs[0], list(outs[1:])


PROJ_CHUNK_COLS = 256


def _col_chunks(n):
    step = min(n, PROJ_CHUNK_COLS)
    return [slice(c0, c0 + step) for c0 in range(0, n, step)]


def _glu_kernel(h_ref, wa_ref, wb_ref, ba_ref, bb_ref, o_ref):
    h = h_ref[...]
    for cols in _col_chunks(o_ref.shape[1]):
        a = jnp.dot(h, wa_ref[:, cols], preferred_element_type=F32) + ba_ref[:, cols]
        b = jnp.dot(h, wb_ref[:, cols], preferred_element_type=F32) + bb_ref[:, cols]
        o_ref[:, cols] = a * jax.nn.sigmoid(b)


def glu_proj(h, w, bias):
    t, d = h.shape
    n = w.shape[1] // 2
    bm = _pick(t, 1024)
    bn = _pick(n, 512)
    nb = n // bn
    bias = bias.reshape(1, 2 * n)
    return pl.pallas_call(
        _glu_kernel,
        grid=(t // bm, nb),
        in_specs=[pl.BlockSpec((bm, d), lambda i, j: (i, 0)),
                  pl.BlockSpec((d, bn), lambda i, j: (0, j)),
                  pl.BlockSpec((d, bn), lambda i, j: (0, j + nb)),
                  pl.BlockSpec((1, bn), lambda i, j: (0, j)),
                  pl.BlockSpec((1, bn), lambda i, j: (0, j + nb))],
        out_specs=pl.BlockSpec((bm, bn), lambda i, j: (i, j)),
        out_shape=jax.ShapeDtypeStruct((t, n), F32),
        compiler_params=_params("parallel", "arbitrary"),
        name="glu_proj",
    )(h, w, w, bias, bias)


def _res_proj_kernel(h_ref, w_ref, b_ref, x_ref, o_ref):
    h = h_ref[...]
    for cols in _col_chunks(o_ref.shape[1]):
        acc = jnp.dot(h, w_ref[:, cols], preferred_element_type=F32)
        o_ref[:, cols] = x_ref[:, cols] + (acc + b_ref[:, cols])


def res_proj(h, w, bias, x):
    t, k = h.shape
    n = w.shape[1]
    bm = _pick(t, 1024)
    bn = _pick(n, 512)
    return pl.pallas_call(
        _res_proj_kernel,
        grid=(t // bm, n // bn),
        in_specs=[pl.BlockSpec((bm, k), lambda i, j: (i, 0)),
                  pl.BlockSpec((k, bn), lambda i, j: (0, j)),
                  pl.BlockSpec((1, bn), lambda i, j: (0, j)),
                  pl.BlockSpec((bm, bn), lambda i, j: (i, j))],
        out_specs=pl.BlockSpec((bm, bn), lambda i, j: (i, j)),
        out_shape=jax.ShapeDtypeStruct((t, n), F32),
        compiler_params=_params("parallel", "arbitrary"),
        name="res_proj",
    )(h, w, bias.reshape(1, n), x)


def _qkv_kernel(*refs, tiles_per_part, n_side):
    h_ref, w_ref, cos_ref, sin_ref = refs[:4]
    side_in = refs[4:4 + n_side]
    o_ref = refs[4 + n_side]
    side_out = refs[5 + n_side:5 + 2 * n_side]
    for src, dst in zip(side_in, side_out):
        dst[...] = src[...].astype(dst.dtype)
    h = h_ref[...]
    classes, rows, bn = o_ref.shape
    rotate = pl.program_id(1) // tiles_per_part < 2
    cos = jnp.where(rotate, cos_ref[...], 1.0)
    sin = jnp.where(rotate, sin_ref[...], 0.0)
    for chunk in _col_chunks(bn):
        acc = jnp.dot(h, w_ref[:, chunk], preferred_element_type=F32)
        for off in range(0, chunk.stop - chunk.start, HEAD_DIM):
            t = acc[:, off:off + HEAD_DIM]
            val = (t * cos + pltpu.roll(t, HEAD_DIM // 2, axis=1) * sin).astype(o_ref.dtype)
            cols = slice(chunk.start + off, chunk.start + off + HEAD_DIM)
            for c in range(classes):
                o_ref[c, :, cols] = val[c * rows:(c + 1) * rows]


def qkv_proj(h, w, group, dilation, cos_full, sin_signed, batch, seq, part_width, side=()):
    t, d = h.shape
    n = 3 * part_width
    bm = _pick(seq, 1024)
    bn = _pick(part_width, 1024)
    pos_tiles = seq // bm
    col_tiles = n // bn
    length = seq // dilation
    classes, rows = max(1, bm // length), min(bm, length)
    tiles_per_class = length // rows

    def out_index(i, j):
        tile = i % pos_tiles
        return (i // pos_tiles, tile // tiles_per_class, tile % tiles_per_class, j)

    kern = functools.partial(_qkv_kernel, tiles_per_part=part_width // bn, n_side=len(side))
    grid = (t // bm, col_tiles)
    side_specs = [_side_cast_specs(sw, layer, grid[0] * grid[1], lambda i, j: i * col_tiles + j)
                  for sw, layer in side]
    outs = pl.pallas_call(
        kern,
        grid=grid,
        in_specs=[pl.BlockSpec((bm, d), lambda i, j: (i, 0), pipeline_mode=pl.Buffered(1)),
                  pl.BlockSpec((d, bn), lambda i, j: (0, group * col_tiles + j)),
                  pl.BlockSpec((bm, HEAD_DIM), lambda i, j: (i % pos_tiles, 0)),
                  pl.BlockSpec((bm, HEAD_DIM), lambda i, j: (i % pos_tiles, 0))]
        + [s[0] for s in side_specs],
        out_specs=[pl.BlockSpec((None, classes, rows, bn), out_index)] + [s[1] for s in side_specs],
        out_shape=[jax.ShapeDtypeStruct((batch, dilation, length, n), BF16)]
        + [s[2] for s in side_specs],
        compiler_params=_params("arbitrary", "arbitrary"),
        name=f"qkv_proj_d{dilation}",
    )(h, w, cos_full, sin_signed, *[sw for sw, _ in side])
    return outs[0], list(outs[1:])


CONV_HALO = 16
CONV_ROWS = 64
CONV_COLS = 256


def _conv_kernel(prev_ref, cur_ref, next_ref, dw_ref, dwb_ref, lng_ref, lnb_ref,
                 o_ref, xs_ref, cv_ref, sh_ref):
    ts, d = cur_ref.shape[1], cur_ref.shape[2]
    i = pl.program_id(1)
    last = pl.num_programs(1) - 1
    xs_ref[0:CONV_HALO, :] = jnp.where(i > 0, prev_ref[0], 0.0)
    xs_ref[CONV_HALO:CONV_HALO + ts, :] = cur_ref[0]
    xs_ref[CONV_HALO + ts:, :] = jnp.where(i < last, next_ref[0], 0.0)

    n_rows = ts // CONV_ROWS
    n_cols = d // CONV_COLS
    shift = CONV_HALO - CONV_WIDTH // 2

    span = sh_ref.shape[1]

    def chunk(idx, carry):
        cols = pl.ds(pl.multiple_of(idx * CONV_COLS, CONV_COLS), CONV_COLS)
        for b in range(V7X_SUBLANES):
            sh_ref[b] = xs_ref[b:b + span, cols]
        for r0 in range(0, ts, CONV_ROWS):
            acc = jnp.zeros((CONV_ROWS, CONV_COLS), F32)
            for k in range(CONV_WIDTH):
                a, b = divmod(k + shift, V7X_SUBLANES)
                row = r0 + a * V7X_SUBLANES
                acc = acc + sh_ref[b, row:row + CONV_ROWS, :] * dw_ref[k:k + 1, cols]
            cv_ref[r0:r0 + CONV_ROWS, cols] = acc + dwb_ref[:, cols]
        return carry

    lax.fori_loop(0, n_cols, chunk, 0)

    def norm(r, carry):
        r0 = pl.multiple_of(r * CONV_ROWS, CONV_ROWS)
        hf = cv_ref[pl.ds(r0, CONV_ROWS), :]
        mu = jnp.mean(hf, axis=-1, keepdims=True)
        cen = hf - mu
        var = jnp.mean(cen * cen, axis=-1, keepdims=True)
        y = cen * lax.rsqrt(var + LN_EPS) * lng_ref[...] + lnb_ref[...]
        o_ref[0, pl.ds(r0, CONV_ROWS), :] = (y * jax.nn.sigmoid(y)).astype(o_ref.dtype)
        return carry

    lax.fori_loop(0, n_rows, norm, 0)


def conv_core(x, dw, dw_b, ln_g, ln_b):
    b, s, d = x.shape
    ts = _pick(s, 256)
    hb = ts // CONV_HALO
    n_halo_blocks = s // CONV_HALO
    span_extra = (CONV_WIDTH - 1 + CONV_HALO - CONV_WIDTH // 2) // V7X_SUBLANES * V7X_SUBLANES
    assert span_extra + V7X_SUBLANES <= 2 * CONV_HALO
    row = lambda v: v.reshape(1, d)
    return pl.pallas_call(
        _conv_kernel,
        grid=(b, s // ts),
        in_specs=[
            pl.BlockSpec((1, CONV_HALO, d), lambda bi, i: (bi, jnp.maximum(i * hb - 1, 0), 0)),
            pl.BlockSpec((1, ts, d), lambda bi, i: (bi, i, 0)),
            pl.BlockSpec((1, CONV_HALO, d),
                         lambda bi, i: (bi, jnp.minimum((i + 1) * hb, n_halo_blocks - 1), 0)),
            pl.BlockSpec((CONV_WIDTH, d), lambda bi, i: (0, 0)),
            pl.BlockSpec((1, d), lambda bi, i: (0, 0)),
            pl.BlockSpec((1, d), lambda bi, i: (0, 0)),
            pl.BlockSpec((1, d), lambda bi, i: (0, 0)),
        ],
        out_specs=pl.BlockSpec((1, ts, d), lambda bi, i: (bi, i, 0)),
        out_shape=jax.ShapeDtypeStruct((b, s, d), BF16),
        scratch_shapes=[pltpu.VMEM((ts + 2 * CONV_HALO, d), F32),
                        pltpu.VMEM((ts, d), F32),
                        pltpu.VMEM((V7X_SUBLANES, ts + span_extra, CONV_COLS), F32)],
        compiler_params=_params("parallel", "arbitrary"),
        name="conv_core",
    )(x, x, x, dw, row(dw_b), row(ln_g), row(ln_b))


ATTN_HEADS_PER_STEP = 8
ATTN_SUB_ROWS = 128


def _band_attn_kernel(q_ref, kp_ref, kc_ref, kn_ref, vp_ref, vc_ref, vn_ref,
                      o_ref, lse_ref, *, radius, length):
    bq = q_ref.shape[0]
    sub = min(bq, ATTN_SUB_ROWS)
    nk = sub + 2 * radius
    j0 = pl.program_id(3) * bq
    qi = lax.broadcasted_iota(jnp.int32, (sub, nk), 0)
    kj = lax.broadcasted_iota(jnp.int32, (sub, nk), 1) - radius
    band = jnp.abs(kj - qi) <= radius
    scale = HEAD_DIM ** -0.5
    for r0 in range(0, bq, sub):
        kabs = kj + (j0 + r0)
        valid = band & (kabs >= 0) & (kabs < length)
        for hh in range(q_ref.shape[1] // HEAD_DIM):
            cols = slice(hh * HEAD_DIM, (hh + 1) * HEAD_DIM)
            q = q_ref[r0:r0 + sub, cols]
            parts = []
            for src, lo in ((kp_ref, vp_ref), 0), ((kc_ref, vc_ref), radius), ((kn_ref, vn_ref), radius + bq):
                a, b = max(r0, lo), min(r0 + nk, lo + src[0].shape[0])
                if a < b:
                    parts.append((src, a - lo, b - lo))
            keys = jnp.concatenate([s[0][a:b, cols] for s, a, b in parts], axis=0)
            vals = jnp.concatenate([s[1][a:b, cols] for s, a, b in parts], axis=0)
            sc = lax.dot_general(q, keys, (((1,), (1,)), ((), ())),
                                 preferred_element_type=F32) * scale
            sc = jnp.where(valid, sc, NEG_INF)
            m = jnp.max(sc, axis=-1, keepdims=True)
            p = jnp.exp(sc - m)
            den = jnp.sum(p, axis=-1, keepdims=True)
            o = jnp.dot(p.astype(BF16), vals, preferred_element_type=F32) / den
            o_ref[r0:r0 + sub, cols] = o
            lse_ref[r0:r0 + sub, cols] = jnp.broadcast_to(m + jnp.log(den), (sub, HEAD_DIM))


def band_attention(qkv, dilation, radius):
    batch, _, length, total = qkv.shape
    width = total // 3
    w = min(ATTN_HEADS_PER_STEP * HEAD_DIM, width)
    bq = _pick(length, 256)
    assert bq % radius == 0 and length % radius == 0
    rb = bq // radius
    n_rblocks = length // radius
    hblocks = width // w

    def cur(part):
        return pl.BlockSpec((None, None, bq, w), lambda b, c, h, i: (b, c, i, part * hblocks + h))

    def prev(part):
        return pl.BlockSpec((None, None, radius, w),
                            lambda b, c, h, i: (b, c, jnp.maximum(i * rb - 1, 0), part * hblocks + h))

    def nxt(part):
        return pl.BlockSpec(
            (None, None, radius, w),
            lambda b, c, h, i: (b, c, jnp.minimum((i + 1) * rb, n_rblocks - 1), part * hblocks + h))

    out_spec = pl.BlockSpec((None, None, bq, w), lambda b, c, h, i: (b, c, i, h))
    kern = functools.partial(_band_attn_kernel, radius=radius, length=length)
    return pl.pallas_call(
        kern,
        grid=(batch, dilation, hblocks, length // bq),
        in_specs=[cur(0), prev(1), cur(1), nxt(1), prev(2), cur(2), nxt(2)],
        out_specs=[out_spec, out_spec],
        out_shape=[jax.ShapeDtypeStruct((batch, dilation, length, width), F32)] * 2,
        compiler_params=_params("parallel", "parallel", "parallel", "arbitrary"),
        name=f"band_attn_d{dilation}",
    )(qkv, qkv, qkv, qkv, qkv, qkv, qkv)


COMBINE_ROWS = 256
COMBINE_COLS = 1024


def _combine_kernel(*refs, dilations):
    n = len(dilations)
    o_refs, l_refs, out_ref, scr_ref = refs[:n], refs[n:2 * n], refs[2 * n], refs[2 * n + 1]
    bm, bn = out_ref.shape
    slabs = bn // V7X_LANES
    for g, d in enumerate(dilations):
        for a, ref in enumerate((o_refs[g], l_refs[g])):
            for c in range(d):
                for s in range(slabs):
                    lanes = slice(s * V7X_LANES, (s + 1) * V7X_LANES)
                    rows = pl.ds(c, bm // d, stride=d) if d > 1 else slice(None)
                    scr_ref[2 * g + a, s, rows, :] = ref[c, :, lanes]
    for s in range(slabs):
        outs = [scr_ref[2 * g, s] for g in range(n)]
        lses = [scr_ref[2 * g + 1, s] for g in range(n)]
        m = functools.reduce(jnp.maximum, lses)
        ws = [jnp.exp(l - m) for l in lses]
        den = functools.reduce(lambda a, b: a + b, ws)
        num = functools.reduce(lambda a, b: a + b, [w * o for w, o in zip(ws, outs)])
        out_ref[:, s * V7X_LANES:(s + 1) * V7X_LANES] = (num / den).astype(out_ref.dtype)


def combine_groups(outs, lses):
    batch, _, _, width = outs[0].shape
    dilations = tuple(o.shape[1] for o in outs)
    seq = outs[0].shape[1] * outs[0].shape[2]
    bm = _pick(seq, COMBINE_ROWS)
    bn = _pick(width, COMBINE_COLS)
    pos_tiles = seq // bm

    def spec(d):
        return pl.BlockSpec((None, d, bm // d, bn), lambda i, j: (i // pos_tiles, 0, i % pos_tiles, j))

    specs = [spec(d) for d in dilations]
    return pl.pallas_call(
        functools.partial(_combine_kernel, dilations=dilations),
        grid=(batch * pos_tiles, width // bn),
        in_specs=specs + specs,
        out_specs=pl.BlockSpec((bm, bn), lambda i, j: (i, j)),
        out_shape=jax.ShapeDtypeStruct((batch * seq, width), BF16),
        scratch_shapes=[pltpu.VMEM((2 * len(outs), bn // V7X_LANES, bm, V7X_LANES), F32)],
        compiler_params=_params("parallel", "arbitrary"),
        name="combine_groups",
    )(*outs, *lses)


CAST_BLOCK_BYTES = 4 * 1024 * 1024


def _cast_kernel(w_ref, o_ref):
    o_ref[...] = w_ref[...].astype(o_ref.dtype)


def cast_layer(w, layer):
    _, r, c = w.shape
    br = _pick(r, max(16, CAST_BLOCK_BYTES // (4 * c)))
    return pl.pallas_call(
        _cast_kernel,
        grid=(r // br,),
        in_specs=[pl.BlockSpec((None, br, c), lambda i: (layer, i, 0))],
        out_specs=pl.BlockSpec((br, c), lambda i: (i, 0)),
        out_shape=jax.ShapeDtypeStruct((r, c), BF16),
        compiler_params=_params("parallel"),
        name="cast_layer",
    )(w)


def _rope_tables(seq):
    inv = ROPE_THETA ** (-jnp.arange(0, HEAD_DIM, 2, dtype=F32) / HEAD_DIM)
    ang = jnp.arange(seq, dtype=F32)[:, None] * inv[None, :]
    cos, sin = jnp.cos(ang), jnp.sin(ang)
    return jnp.concatenate([cos, cos], axis=-1), jnp.concatenate([-sin, sin], axis=-1)


def kernel(x, ffn1_norm, ffn1_wg, ffn1_wu, ffn1_wd, mix_norm, ffn2_norm, ffn2_wg, ffn2_wu, ffn2_wd, conv_pw1, conv_pw1_b, conv_dw, conv_dw_b, conv_ln_g, conv_ln_b, conv_pw2, conv_pw2_b, attn_wqkv, attn_wo, final_norm):
    batch, seq, d = x.shape
    assert ffn1_norm.shape[0] == 2 and conv_pw1.shape[0] == 1 and attn_wqkv.shape[0] == 1
    width = attn_wo.shape[1]
    cos_full, sin_signed = _rope_tables(seq)
    xt = x.reshape(batch * seq, d)

    w_ffn = [cast_layer(w, 0) for w in (ffn1_wg, ffn1_wu, ffn1_wd)]
    xt, (pw1, pw2, *w_ffn) = ffn(
        xt, ffn1_norm[0], *w_ffn,
        side=[(conv_pw1, 0), (conv_pw2, 0), (ffn2_wg, 0), (ffn2_wu, 0), (ffn2_wd, 0)])
    h = rmsnorm(xt, mix_norm[0], BF16)
    glu = glu_proj(h, pw1, conv_pw1_b[0])
    act = conv_core(glu.reshape(batch, seq, d), conv_dw[0], conv_dw_b[0], conv_ln_g[0], conv_ln_b[0])
    xt = res_proj(act.reshape(batch * seq, d), pw2, conv_pw2_b[0], xt)
    xt, w_ffn = ffn(xt, ffn2_norm[0], *w_ffn, side=[(ffn1_wg, 1), (ffn1_wu, 1), (ffn1_wd, 1)])

    xt, (wqkv, wo) = ffn(xt, ffn1_norm[1], *w_ffn, side=[(attn_wqkv, 0), (attn_wo, 0)])
    dilations = [dil for _, dil in ATTN_GROUPS]
    hs = rmsnorm_classes(xt, mix_norm[1], batch, seq, dilations)
    outs, lses, w_ffn = [], [], []
    for g, ((window, dil), w_next) in enumerate(zip(ATTN_GROUPS, (ffn2_wg, ffn2_wu, ffn2_wd))):
        by_class = lambda tab: tab.reshape(seq // dil, dil, HEAD_DIM).swapaxes(0, 1).reshape(seq, HEAD_DIM)
        qkv, cast = qkv_proj(hs[g], wqkv, g, dil, by_class(cos_full), by_class(sin_signed),
                             batch, seq, width, side=[(w_next, 1)])
        w_ffn += cast
        o, l = band_attention(qkv, dil, window // (2 * dil))
        outs.append(o)
        lses.append(l)
    mixed = combine_groups(outs, lses)
    xt = res_proj(mixed, wo, jnp.zeros((d,), F32), xt)
    xt, _ = ffn(xt, ffn2_norm[1], *w_ffn)
    return rmsnorm(xt, final_norm, F32).reshape(batch, seq, d)
```

```python
import functools

import jax
import jax.numpy as jnp
from jax import lax
from jax.experimental import pallas as pl
from jax.experimental.pallas import tpu as pltpu

F32 = jnp.float32
BF16 = jnp.bfloat16

HEAD_DIM = 128
ATTN_GROUPS = ((128, 1), (512, 4), (2048, 16))
CONV_WIDTH = 31
ROPE_THETA = 10000.0
RMS_EPS = 1e-6
LN_EPS = 1e-5
NEG_INF = -1e30

V7X_LANES = 128
V7X_SUBLANES = 8
V7X_VMEM_LIMIT_BYTES = 56 * 1024 * 1024


def _params(*semantics):
    return pltpu.CompilerParams(
        dimension_semantics=semantics, vmem_limit_bytes=V7X_VMEM_LIMIT_BYTES)


def _pick(n, pref):
    if n <= pref:
        return n
    t = 1 << (pref.bit_length() - 1)
    while n % t:
        t //= 2
    return t


def _rms_rows(x, gain):
    ms = jnp.mean(x * x, axis=-1, keepdims=True)
    return x * lax.rsqrt(ms + RMS_EPS) * gain


def _rmsnorm_kernel(x_ref, g_ref, o_ref):
    o_ref[...] = _rms_rows(x_ref[...], g_ref[...]).astype(o_ref.dtype)


def rmsnorm(x, gain, out_dtype):
    t, d = x.shape
    bm = _pick(t, 256)
    return pl.pallas_call(
        _rmsnorm_kernel,
        grid=(t // bm,),
        in_specs=[pl.BlockSpec((bm, d), lambda i: (i, 0)),
                  pl.BlockSpec((1, d), lambda i: (0, 0))],
        out_specs=pl.BlockSpec((bm, d), lambda i: (i, 0)),
        out_shape=jax.ShapeDtypeStruct((t, d), out_dtype),
        compiler_params=_params("parallel"),
        name="rmsnorm",
    )(x, gain.reshape(1, d))


def _rmsnorm_classes_kernel(x_ref, g_ref, *refs, dilations):
    outs, scr_ref = refs[:-1], refs[-1]
    y = _rms_rows(x_ref[...], g_ref[...])
    bm, d = y.shape
    slabs = d // V7X_LANES
    if any(dil > 1 for dil in dilations):
        for s in range(slabs):
            scr_ref[s] = y[:, s * V7X_LANES:(s + 1) * V7X_LANES]
    for o_ref, dil in zip(outs, dilations):
        if dil == 1:
            o_ref[0] = y.astype(o_ref.dtype)
            continue
        for c in range(dil):
            for s in range(slabs):
                o_ref[c, :, s * V7X_LANES:(s + 1) * V7X_LANES] = (
                    scr_ref[s, pl.ds(c, bm // dil, stride=dil), :].astype(o_ref.dtype))


def rmsnorm_classes(x, gain, batch, seq, dilations):
    t, d = x.shape
    bm = _pick(seq, 256)
    pos_tiles = seq // bm
    out_specs = [pl.BlockSpec((None, dil, bm // dil, d), lambda i: (i // pos_tiles, 0, i % pos_tiles, 0))
                 for dil in dilations]
    outs = pl.pallas_call(
        functools.partial(_rmsnorm_classes_kernel, dilations=tuple(dilations)),
        grid=(t // bm,),
        in_specs=[pl.BlockSpec((bm, d), lambda i: (i, 0)),
                  pl.BlockSpec((1, d), lambda i: (0, 0))],
        out_specs=out_specs,
        out_shape=[jax.ShapeDtypeStruct((batch, dil, seq // dil, d), BF16) for dil in dilations],
        scratch_shapes=[pltpu.VMEM((d // V7X_LANES, bm, V7X_LANES), F32)],
        compiler_params=_params("parallel"),
        name="rmsnorm_classes",
    )(x, gain.reshape(1, d))
    return [o.reshape(t, d) for o in outs]


FFN_LOAD_ROWS = 128
FFN_DOWN_COLS = 512


def _x_chunk_copy(x_hbm, o_ref, sems, tile, chunk):
    rows = o_ref.shape[0]
    src = x_hbm.at[pl.ds(tile * rows + chunk * FFN_LOAD_ROWS, FFN_LOAD_ROWS)]
    return pltpu.make_async_copy(src, o_ref.at[pl.ds(chunk * FFN_LOAD_ROWS, FFN_LOAD_ROWS)],
                                 sems.at[chunk])


def _ffn_kernel(*refs, n_side, final_norm):
    x_hbm, g_ref, wg_ref, wu_ref, wd_ref = refs[:5]
    n_in = 6 if final_norm else 5
    side_in = refs[n_in:n_in + n_side]
    o_ref = refs[n_in + n_side]
    side_out = refs[n_in + n_side + 1:n_in + 2 * n_side + 1]
    h_ref, sems = refs[n_in + 2 * n_side + 1:]
    bm, d = o_ref.shape
    n_chunks = bm // FFN_LOAD_ROWS

    @pl.when(pl.program_id(1) == 0)
    def _():
        for c in range(n_chunks):
            _x_chunk_copy(x_hbm, o_ref, sems, pl.program_id(0), c).start()
        for c in range(n_chunks):
            _x_chunk_copy(x_hbm, o_ref, sems, pl.program_id(0), c).wait()
            rows = pl.ds(c * FFN_LOAD_ROWS, FFN_LOAD_ROWS)
            h_ref[rows, :] = _rms_rows(o_ref[rows, :], g_ref[...]).astype(BF16)

    h = h_ref[...]
    gate = jnp.dot(h, wg_ref[...], preferred_element_type=F32)
    up = jnp.dot(h, wu_ref[...], preferred_element_type=F32)
    act = (gate * jax.nn.sigmoid(gate) * up * 0.5).astype(BF16)
    step = min(d, FFN_DOWN_COLS)
    for c0 in range(0, d, step):
        cols = slice(c0, c0 + step)
        o_ref[:, cols] += jnp.dot(act, wd_ref[:, cols], preferred_element_type=F32)

    for src, dst in zip(side_in, side_out):
        dst[...] = src[...].astype(dst.dtype)

    if final_norm:
        @pl.when(pl.program_id(1) == pl.num_programs(1) - 1)
        def _():
            for c in range(n_chunks):
                rows = pl.ds(c * FFN_LOAD_ROWS, FFN_LOAD_ROWS)
                o_ref[rows, :] = _rms_rows(o_ref[rows, :], refs[5][...])


def _side_cast_specs(w, layer, n_steps, step_of):
    _, r, c = w.shape
    rb = 16
    while r // rb > n_steps:
        rb *= 2
    assert r % rb == 0
    last = r // rb - 1
    in_spec = pl.BlockSpec((None, rb, c), lambda *g: (layer, jnp.minimum(step_of(*g), last), 0))
    out_spec = pl.BlockSpec((rb, c), lambda *g: (jnp.minimum(step_of(*g), last), 0))
    return in_spec, out_spec, jax.ShapeDtypeStruct((r, c), BF16)


def ffn(x, gain, wg, wu, wd, side=(), final_gain=None):
    t, d = x.shape
    f = wg.shape[1]
    bm = _pick(t, 1024)
    bf = _pick(f, 256)
    assert bm % FFN_LOAD_ROWS == 0 and d % min(d, FFN_DOWN_COLS) == 0
    grid = (t // bm, f // bf)
    side_specs = [_side_cast_specs(w, layer, grid[0] * grid[1], lambda i, j: i * grid[1] + j)
                  for w, layer in side]
    extra = [] if final_gain is None else [final_gain.reshape(1, d)]
    outs = pl.pallas_call(
        functools.partial(_ffn_kernel, n_side=len(side), final_norm=final_gain is not None),
        grid=grid,
        in_specs=[pl.BlockSpec(memory_space=pl.ANY),
                  pl.BlockSpec((1, d), lambda i, j: (0, 0)),
                  pl.BlockSpec((d, bf), lambda i, j: (0, j)),
                  pl.BlockSpec((d, bf), lambda i, j: (0, j)),
                  pl.BlockSpec((bf, d), lambda i, j: (j, 0))]
        + [pl.BlockSpec((1, d), lambda i, j: (0, 0))] * len(extra) + [s[0] for s in side_specs],
        out_specs=[pl.BlockSpec((bm, d), lambda i, j: (i, 0), pipeline_mode=pl.Buffered(1))]
        + [s[1] for s in side_specs],
        out_shape=[jax.ShapeDtypeStruct((t, d), F32)] + [s[2] for s in side_specs],
        scratch_shapes=[pltpu.VMEM((bm, d), BF16),
                        pltpu.SemaphoreType.DMA((bm // FFN_LOAD_ROWS,))],
        compiler_params=_params("arbitrary", "arbitrary"),
        name="ffn",
    )(x, gain.reshape(1, d), wg, wu, wd, *extra, *[w for w, _ in side])
    return outs[0], list(outs[1:])


PROJ_CHUNK_COLS = 256


def _col_chunks(n):
    step = min(n, PROJ_CHUNK_COLS)
    return [slice(c0, c0 + step) for c0 in range(0, n, step)]


def _glu_kernel(h_ref, wa_ref, wb_ref, ba_ref, bb_ref, o_ref):
    h = h_ref[...]
    for cols in _col_chunks(o_ref.shape[1]):
        a = jnp.dot(h, wa_ref[:, cols], preferred_element_type=F32) + ba_ref[:, cols]
        b = jnp.dot(h, wb_ref[:, cols], preferred_element_type=F32) + bb_ref[:, cols]
        o_ref[:, cols] = a * jax.nn.sigmoid(b)


def glu_proj(h, w, bias):
    t, d = h.shape
    n = w.shape[1] // 2
    bm = _pick(t, 1024)
    bn = _pick(n, 512)
    nb = n // bn
    bias = bias.reshape(1, 2 * n)
    return pl.pallas_call(
        _glu_kernel,
        grid=(t // bm, nb),
        in_specs=[pl.BlockSpec((bm, d), lambda i, j: (i, 0)),
                  pl.BlockSpec((d, bn), lambda i, j: (0, j)),
                  pl.BlockSpec((d, bn), lambda i, j: (0, j + nb)),
                  pl.BlockSpec((1, bn), lambda i, j: (0, j)),
                  pl.BlockSpec((1, bn), lambda i, j: (0, j + nb))],
        out_specs=pl.BlockSpec((bm, bn), lambda i, j: (i, j)),
        out_shape=jax.ShapeDtypeStruct((t, n), F32),
        compiler_params=_params("parallel", "arbitrary"),
        name="glu_proj",
    )(h, w, w, bias, bias)


def _res_proj_kernel(h_ref, w_ref, b_ref, x_ref, o_ref):
    h = h_ref[...]
    for cols in _col_chunks(o_ref.shape[1]):
        acc = jnp.dot(h, w_ref[:, cols], preferred_element_type=F32)
        o_ref[:, cols] = x_ref[:, cols] + (acc + b_ref[:, cols])


def res_proj(h, w, bias, x):
    t, k = h.shape
    n = w.shape[1]
    bm = _pick(t, 1024)
    bn = _pick(n, 512)
    return pl.pallas_call(
        _res_proj_kernel,
        grid=(t // bm, n // bn),
        in_specs=[pl.BlockSpec((bm, k), lambda i, j: (i, 0)),
                  pl.BlockSpec((k, bn), lambda i, j: (0, j)),
                  pl.BlockSpec((1, bn), lambda i, j: (0, j)),
                  pl.BlockSpec((bm, bn), lambda i, j: (i, j))],
        out_specs=pl.BlockSpec((bm, bn), lambda i, j: (i, j)),
        out_shape=jax.ShapeDtypeStruct((t, n), F32),
        compiler_params=_params("parallel", "arbitrary"),
        name="res_proj",
    )(h, w, bias.reshape(1, n), x)


def _qkv_kernel(*refs, tiles_per_part, n_side):
    h_ref, w_ref, cos_ref, sin_ref = refs[:4]
    side_in = refs[4:4 + n_side]
    o_ref = refs[4 + n_side]
    side_out = refs[5 + n_side:5 + 2 * n_side]
    for src, dst in zip(side_in, side_out):
        dst[...] = src[...].astype(dst.dtype)
    h = h_ref[...]
    classes, rows, bn = o_ref.shape
    rotate = pl.program_id(1) // tiles_per_part < 2
    cos = jnp.where(rotate, cos_ref[...], 1.0)
    sin = jnp.where(rotate, sin_ref[...], 0.0)
    for chunk in _col_chunks(bn):
        acc = jnp.dot(h, w_ref[:, chunk], preferred_element_type=F32)
        for off in range(0, chunk.stop - chunk.start, HEAD_DIM):
            t = acc[:, off:off + HEAD_DIM]
            val = (t * cos + pltpu.roll(t, HEAD_DIM // 2, axis=1) * sin).astype(o_ref.dtype)
            cols = slice(chunk.start + off, chunk.start + off + HEAD_DIM)
            for c in range(classes):
                o_ref[c, :, cols] = val[c * rows:(c + 1) * rows]


def qkv_proj(h, w, group, dilation, cos_full, sin_signed, batch, seq, part_width, side=()):
    t, d = h.shape
    n = 3 * part_width
    bm = _pick(seq, 1024)
    bn = _pick(part_width, 1024)
    pos_tiles = seq // bm
    col_tiles = n // bn
    length = seq // dilation
    classes, rows = max(1, bm // length), min(bm, length)
    tiles_per_class = length // rows

    def out_index(i, j):
        tile = i % pos_tiles
        return (i // pos_tiles, tile // tiles_per_class, tile % tiles_per_class, j)

    kern = functools.partial(_qkv_kernel, tiles_per_part=part_width // bn, n_side=len(side))
    grid = (t // bm, col_tiles)
    side_specs = [_side_cast_specs(sw, layer, grid[0] * grid[1], lambda i, j: i * col_tiles + j)
                  for sw, layer in side]
    outs = pl.pallas_call(
        kern,
        grid=grid,
        in_specs=[pl.BlockSpec((bm, d), lambda i, j: (i, 0), pipeline_mode=pl.Buffered(1)),
                  pl.BlockSpec((d, bn), lambda i, j: (0, group * col_tiles + j)),
                  pl.BlockSpec((bm, HEAD_DIM), lambda i, j: (i % pos_tiles, 0)),
                  pl.BlockSpec((bm, HEAD_DIM), lambda i, j: (i % pos_tiles, 0))]
        + [s[0] for s in side_specs],
        out_specs=[pl.BlockSpec((None, classes, rows, bn), out_index)] + [s[1] for s in side_specs],
        out_shape=[jax.ShapeDtypeStruct((batch, dilation, length, n), BF16)]
        + [s[2] for s in side_specs],
        compiler_params=_params("arbitrary", "arbitrary"),
        name=f"qkv_proj_d{dilation}",
    )(h, w, cos_full, sin_signed, *[sw for sw, _ in side])
    return outs[0], list(outs[1:])


CONV_HALO = 16
CONV_ROWS = 64
CONV_COLS = 256


def _conv_kernel(prev_ref, cur_ref, next_ref, dw_ref, dwb_ref, lng_ref, lnb_ref,
                 o_ref, xs_ref, cv_ref, sh_ref):
    ts, d = cur_ref.shape[1], cur_ref.shape[2]
    i = pl.program_id(1)
    last = pl.num_programs(1) - 1
    xs_ref[0:CONV_HALO, :] = jnp.where(i > 0, prev_ref[0], 0.0)
    xs_ref[CONV_HALO:CONV_HALO + ts, :] = cur_ref[0]
    xs_ref[CONV_HALO + ts:, :] = jnp.where(i < last, next_ref[0], 0.0)

    n_rows = ts // CONV_ROWS
    n_cols = d // CONV_COLS
    shift = CONV_HALO - CONV_WIDTH // 2

    span = sh_ref.shape[1]

    def chunk(idx, carry):
        cols = pl.ds(pl.multiple_of(idx * CONV_COLS, CONV_COLS), CONV_COLS)
        for b in range(V7X_SUBLANES):
            sh_ref[b] = xs_ref[b:b + span, cols]
        for r0 in range(0, ts, CONV_ROWS):
            acc = jnp.zeros((CONV_ROWS, CONV_COLS), F32)
            for k in range(CONV_WIDTH):
                a, b = divmod(k + shift, V7X_SUBLANES)
                row = r0 + a * V7X_SUBLANES
                acc = acc + sh_ref[b, row:row + CONV_ROWS, :] * dw_ref[k:k + 1, cols]
            cv_ref[r0:r0 + CONV_ROWS, cols] = acc + dwb_ref[:, cols]
        return carry

    lax.fori_loop(0, n_cols, chunk, 0)

    def norm(r, carry):
        r0 = pl.multiple_of(r * CONV_ROWS, CONV_ROWS)
        hf = cv_ref[pl.ds(r0, CONV_ROWS), :]
        mu = jnp.mean(hf, axis=-1, keepdims=True)
        cen = hf - mu
        var = jnp.mean(cen * cen, axis=-1, keepdims=True)
        y = cen * lax.rsqrt(var + LN_EPS) * lng_ref[...] + lnb_ref[...]
        o_ref[0, pl.ds(r0, CONV_ROWS), :] = (y * jax.nn.sigmoid(y)).astype(o_ref.dtype)
        return carry

    lax.fori_loop(0, n_rows, norm, 0)


def conv_core(x, dw, dw_b, ln_g, ln_b):
    b, s, d = x.shape
    ts = _pick(s, 256)
    hb = ts // CONV_HALO
    n_halo_blocks = s // CONV_HALO
    span_extra = (CONV_WIDTH - 1 + CONV_HALO - CONV_WIDTH // 2) // V7X_SUBLANES * V7X_SUBLANES
    assert span_extra + V7X_SUBLANES <= 2 * CONV_HALO
    row = lambda v: v.reshape(1, d)
    return pl.pallas_call(
        _conv_kernel,
        grid=(b, s // ts),
        in_specs=[
            pl.BlockSpec((1, CONV_HALO, d), lambda bi, i: (bi, jnp.maximum(i * hb - 1, 0), 0)),
            pl.BlockSpec((1, ts, d), lambda bi, i: (bi, i, 0)),
            pl.BlockSpec((1, CONV_HALO, d),
                         lambda bi, i: (bi, jnp.minimum((i + 1) * hb, n_halo_blocks - 1), 0)),
            pl.BlockSpec((CONV_WIDTH, d), lambda bi, i: (0, 0)),
            pl.BlockSpec((1, d), lambda bi, i: (0, 0)),
            pl.BlockSpec((1, d), lambda bi, i: (0, 0)),
            pl.BlockSpec((1, d), lambda bi, i: (0, 0)),
        ],
        out_specs=pl.BlockSpec((1, ts, d), lambda bi, i: (bi, i, 0)),
        out_shape=jax.ShapeDtypeStruct((b, s, d), BF16),
        scratch_shapes=[pltpu.VMEM((ts + 2 * CONV_HALO, d), F32),
                        pltpu.VMEM((ts, d), F32),
                        pltpu.VMEM((V7X_SUBLANES, ts + span_extra, CONV_COLS), F32)],
        compiler_params=_params("parallel", "arbitrary"),
        name="conv_core",
    )(x, x, x, dw, row(dw_b), row(ln_g), row(ln_b))


ATTN_HEADS_PER_STEP = 8
ATTN_SUB_ROWS = 128


def _scatter_classes(src_ref, dst_ref, dilation):
    rows = dst_ref.shape[1]
    for c in range(dilation):
        for s in range(dst_ref.shape[0]):
            lanes = slice(s * V7X_LANES, (s + 1) * V7X_LANES)
            dst_ref[s, pl.ds(c, rows // dilation, stride=dilation), :] = src_ref[c, :, lanes]


def _band_attn_kernel(*refs, radius, length, merge_dilations):
    q_ref, kp_ref, kc_ref, kn_ref, vp_ref, vc_ref, vn_ref = refs[:7]
    n_merge = len(merge_dilations)
    merge_o, merge_l = refs[7:7 + n_merge], refs[7 + n_merge:7 + 2 * n_merge]
    if n_merge:
        out_ref, scr_ref = refs[7 + 2 * n_merge:]
        for g, dil in enumerate(merge_dilations):
            _scatter_classes(merge_o[g], scr_ref.at[2 * g], dil)
            _scatter_classes(merge_l[g], scr_ref.at[2 * g + 1], dil)
    else:
        o_ref, lse_ref = refs[7:]
    bq = q_ref.shape[0]
    sub = min(bq, ATTN_SUB_ROWS)
    nk = sub + 2 * radius
    j0 = pl.program_id(3) * bq
    qi = lax.broadcasted_iota(jnp.int32, (sub, nk), 0)
    kj = lax.broadcasted_iota(jnp.int32, (sub, nk), 1) - radius
    band = jnp.abs(kj - qi) <= radius
    scale = HEAD_DIM ** -0.5
    for r0 in range(0, bq, sub):
        kabs = kj + (j0 + r0)
        valid = band & (kabs >= 0) & (kabs < length)
        for hh in range(q_ref.shape[1] // HEAD_DIM):
            cols = slice(hh * HEAD_DIM, (hh + 1) * HEAD_DIM)
            q = q_ref[r0:r0 + sub, cols]
            parts = []
            for src, lo in ((kp_ref, vp_ref), 0), ((kc_ref, vc_ref), radius), ((kn_ref, vn_ref), radius + bq):
                a, b = max(r0, lo), min(r0 + nk, lo + src[0].shape[0])
                if a < b:
                    parts.append((src, a - lo, b - lo))
            keys = jnp.concatenate([s[0][a:b, cols] for s, a, b in parts], axis=0)
            vals = jnp.concatenate([s[1][a:b, cols] for s, a, b in parts], axis=0)
            sc = lax.dot_general(q, keys, (((1,), (1,)), ((), ())),
                                 preferred_element_type=F32) * scale
            sc = jnp.where(valid, sc, NEG_INF)
            m = jnp.max(sc, axis=-1, keepdims=True)
            p = jnp.exp(sc - m)
            den = jnp.sum(p, axis=-1, keepdims=True)
            o = jnp.dot(p.astype(BF16), vals, preferred_element_type=F32) / den
            lse = jnp.broadcast_to(m + jnp.log(den), (sub, HEAD_DIM))
            if not n_merge:
                o_ref[r0:r0 + sub, cols] = o
                lse_ref[r0:r0 + sub, cols] = lse
                continue
            outs = [o] + [scr_ref[2 * g, hh, r0:r0 + sub, :] for g in range(n_merge)]
            lses = [lse] + [scr_ref[2 * g + 1, hh, r0:r0 + sub, :] for g in range(n_merge)]
            top = functools.reduce(jnp.maximum, lses)
            ws = [jnp.exp(l - top) for l in lses]
            total = functools.reduce(lambda x, y: x + y, ws)
            mixed = functools.reduce(lambda x, y: x + y, [w * v for w, v in zip(ws, outs)])
            out_ref[r0:r0 + sub, cols] = (mixed / total).astype(out_ref.dtype)


def band_attention(qkv, dilation, radius, merge=None):
    batch, _, length, total = qkv.shape
    width = total // 3
    w = min(ATTN_HEADS_PER_STEP * HEAD_DIM, width)
    bq = _pick(length, 256)
    assert bq % radius == 0 and length % radius == 0
    rb = bq // radius
    n_rblocks = length // radius
    hblocks = width // w

    def cur(part):
        return pl.BlockSpec((None, None, bq, w), lambda b, c, h, i: (b, c, i, part * hblocks + h))

    def prev(part):
        return pl.BlockSpec((None, None, radius, w),
                            lambda b, c, h, i: (b, c, jnp.maximum(i * rb - 1, 0), part * hblocks + h))

    def nxt(part):
        return pl.BlockSpec(
            (None, None, radius, w),
            lambda b, c, h, i: (b, c, jnp.minimum((i + 1) * rb, n_rblocks - 1), part * hblocks + h))

    in_specs = [cur(0), prev(1), cur(1), nxt(1), prev(2), cur(2), nxt(2)]
    operands = [qkv] * 7
    if merge is None:
        merge_dilations = ()
        out_spec = pl.BlockSpec((None, None, bq, w), lambda b, c, h, i: (b, c, i, h))
        out_specs = [out_spec, out_spec]
        out_shape = [jax.ShapeDtypeStruct((batch, dilation, length, width), F32)] * 2
        scratch = []
    else:
        assert dilation == 1
        merge_dilations = tuple(o.shape[1] for o in merge[0])
        tiles = length // bq
        for dil in merge_dilations:
            in_specs.append(pl.BlockSpec((None, dil, bq // dil, w), lambda b, c, h, i: (b, 0, i, h)))
        in_specs += in_specs[7:]
        operands += list(merge[0]) + list(merge[1])
        out_specs = pl.BlockSpec((bq, w), lambda b, c, h, i: (b * tiles + i, h))
        out_shape = jax.ShapeDtypeStruct((batch * length, width), BF16)
        scratch = [pltpu.VMEM((2 * len(merge_dilations), w // V7X_LANES, bq, V7X_LANES), F32)]
    kern = functools.partial(_band_attn_kernel, radius=radius, length=length,
                             merge_dilations=merge_dilations)
    return pl.pallas_call(
        kern,
        grid=(batch, dilation, hblocks, length // bq),
        in_specs=in_specs,
        out_specs=out_specs,
        out_shape=out_shape,
        scratch_shapes=scratch,
        compiler_params=_params("parallel", "parallel", "parallel", "arbitrary"),
        name=f"band_attn_d{dilation}",
    )(*operands)


CAST_BLOCK_BYTES = 4 * 1024 * 1024


def _cast_kernel(w_ref, o_ref):
    o_ref[...] = w_ref[...].astype(o_ref.dtype)


def cast_layer(w, layer):
    _, r, c = w.shape
    br = _pick(r, max(16, CAST_BLOCK_BYTES // (4 * c)))
    return pl.pallas_call(
        _cast_kernel,
        grid=(r // br,),
        in_specs=[pl.BlockSpec((None, br, c), lambda i: (layer, i, 0))],
        out_specs=pl.BlockSpec((br, c), lambda i: (i, 0)),
        out_shape=jax.ShapeDtypeStruct((r, c), BF16),
        compiler_params=_params("parallel"),
        name="cast_layer",
    )(w)


def _rope_tables(seq):
    inv = ROPE_THETA ** (-jnp.arange(0, HEAD_DIM, 2, dtype=F32) / HEAD_DIM)
    ang = jnp.arange(seq, dtype=F32)[:, None] * inv[None, :]
    cos, sin = jnp.cos(ang), jnp.sin(ang)
    return jnp.concatenate([cos, cos], axis=-1), jnp.concatenate([-sin, sin], axis=-1)


def kernel(x, ffn1_norm, ffn1_wg, ffn1_wu, ffn1_wd, mix_norm, ffn2_norm, ffn2_wg, ffn2_wu, ffn2_wd, conv_pw1, conv_pw1_b, conv_dw, conv_dw_b, conv_ln_g, conv_ln_b, conv_pw2, conv_pw2_b, attn_wqkv, attn_wo, final_norm):
    batch, seq, d = x.shape
    assert ffn1_norm.shape[0] == 2 and conv_pw1.shape[0] == 1 and attn_wqkv.shape[0] == 1
    width = attn_wo.shape[1]
    cos_full, sin_signed = _rope_tables(seq)
    xt = x.reshape(batch * seq, d)

    w_ffn = [cast_layer(w, 0) for w in (ffn1_wg, ffn1_wu, ffn1_wd)]
    xt, (pw1, pw2, *w_ffn) = ffn(
        xt, ffn1_norm[0], *w_ffn,
        side=[(conv_pw1, 0), (conv_pw2, 0), (ffn2_wg, 0), (ffn2_wu, 0), (ffn2_wd, 0)])
    h = rmsnorm(xt, mix_norm[0], BF16)
    glu = glu_proj(h, pw1, conv_pw1_b[0])
    act = conv_core(glu.reshape(batch, seq, d), conv_dw[0], conv_dw_b[0], conv_ln_g[0], conv_ln_b[0])
    xt = res_proj(act.reshape(batch * seq, d), pw2, conv_pw2_b[0], xt)
    xt, w_ffn = ffn(xt, ffn2_norm[0], *w_ffn, side=[(ffn1_wg, 1), (ffn1_wu, 1), (ffn1_wd, 1)])

    xt, (wqkv, wo) = ffn(xt, ffn1_norm[1], *w_ffn, side=[(attn_wqkv, 0), (attn_wo, 0)])
    dilations = [dil for _, dil in ATTN_GROUPS]
    hs = rmsnorm_classes(xt, mix_norm[1], batch, seq, dilations)
    order = sorted(range(len(ATTN_GROUPS)), key=lambda g: -ATTN_GROUPS[g][1])
    assert ATTN_GROUPS[order[-1]][1] == 1
    outs, lses, w_ffn = [], [], []
    for g, w_next in zip(order, (ffn2_wg, ffn2_wu, ffn2_wd)):
        window, dil = ATTN_GROUPS[g]
        by_class = lambda tab: tab.reshape(seq // dil, dil, HEAD_DIM).swapaxes(0, 1).reshape(seq, HEAD_DIM)
        qkv, cast = qkv_proj(hs[g], wqkv, g, dil, by_class(cos_full), by_class(sin_signed),
                             batch, seq, width, side=[(w_next, 1)])
        w_ffn += cast
        if g != order[-1]:
            o, l = band_attention(qkv, dil, window // (2 * dil))
            outs.append(o)
            lses.append(l)
        else:
            mixed = band_attention(qkv, dil, window // (2 * dil), merge=(outs, lses))
    xt = res_proj(mixed, wo, jnp.zeros((d,), F32), xt)
    xt, _ = ffn(xt, ffn2_norm[1], *w_ffn, final_gain=final_norm)
    return xt.reshape(batch, seq, d)
```

```python
import functools

import jax
import jax.numpy as jnp
from jax import lax
from jax.experimental import pallas as pl
from jax.experimental.pallas import tpu as pltpu

F32 = jnp.float32
BF16 = jnp.bfloat16

HEAD_DIM = 128
ATTN_GROUPS = ((128, 1), (512, 4), (2048, 16))
CONV_WIDTH = 31
ROPE_THETA = 10000.0
RMS_EPS = 1e-6
LN_EPS = 1e-5
NEG_INF = -1e30

V7X_LANES = 128
V7X_SUBLANES = 8
V7X_VMEM_LIMIT_BYTES = 56 * 1024 * 1024


def _params(*semantics):
    return pltpu.CompilerParams(
        dimension_semantics=semantics, vmem_limit_bytes=V7X_VMEM_LIMIT_BYTES)


def _pick(n, pref):
    if n <= pref:
        return n
    t = 1 << (pref.bit_length() - 1)
    while n % t:
        t //= 2
    return t


def _rms_rows(x, gain):
    ms = jnp.mean(x * x, axis=-1, keepdims=True)
    return x * lax.rsqrt(ms + RMS_EPS) * gain


def _rmsnorm_kernel(x_ref, g_ref, o_ref):
    o_ref[...] = _rms_rows(x_ref[...], g_ref[...]).astype(o_ref.dtype)


def rmsnorm(x, gain, out_dtype):
    t, d = x.shape
    bm = _pick(t, 256)
    return pl.pallas_call(
        _rmsnorm_kernel,
        grid=(t // bm,),
        in_specs=[pl.BlockSpec((bm, d), lambda i: (i, 0)),
                  pl.BlockSpec((1, d), lambda i: (0, 0))],
        out_specs=pl.BlockSpec((bm, d), lambda i: (i, 0)),
        out_shape=jax.ShapeDtypeStruct((t, d), out_dtype),
        compiler_params=_params("parallel"),
        name="rmsnorm",
    )(x, gain.reshape(1, d))


def _rmsnorm_classes_kernel(x_ref, g_ref, *refs, dilations):
    outs, scr_ref = refs[:-1], refs[-1]
    y = _rms_rows(x_ref[...], g_ref[...])
    bm, d = y.shape
    slabs = d // V7X_LANES
    if any(dil > 1 for dil in dilations):
        for s in range(slabs):
            scr_ref[s] = y[:, s * V7X_LANES:(s + 1) * V7X_LANES]
    for o_ref, dil in zip(outs, dilations):
        if dil == 1:
            o_ref[0] = y.astype(o_ref.dtype)
            continue
        for c in range(dil):
            for s in range(slabs):
                o_ref[c, :, s * V7X_LANES:(s + 1) * V7X_LANES] = (
                    scr_ref[s, pl.ds(c, bm // dil, stride=dil), :].astype(o_ref.dtype))


def rmsnorm_classes(x, gain, batch, seq, dilations):
    t, d = x.shape
    bm = _pick(seq, 256)
    pos_tiles = seq // bm
    out_specs = [pl.BlockSpec((None, dil, bm // dil, d), lambda i: (i // pos_tiles, 0, i % pos_tiles, 0))
                 for dil in dilations]
    outs = pl.pallas_call(
        functools.partial(_rmsnorm_classes_kernel, dilations=tuple(dilations)),
        grid=(t // bm,),
        in_specs=[pl.BlockSpec((bm, d), lambda i: (i, 0)),
                  pl.BlockSpec((1, d), lambda i: (0, 0))],
        out_specs=out_specs,
        out_shape=[jax.ShapeDtypeStruct((batch, dil, seq // dil, d), BF16) for dil in dilations],
        scratch_shapes=[pltpu.VMEM((d // V7X_LANES, bm, V7X_LANES), F32)],
        compiler_params=_params("parallel"),
        name="rmsnorm_classes",
    )(x, gain.reshape(1, d))
    return [o.reshape(t, d) for o in outs]


FFN_LOAD_ROWS = 128
FFN_DOWN_COLS = 512


def _x_chunk_copy(x_hbm, o_ref, sems, tile, chunk):
    rows = o_ref.shape[0]
    src = x_hbm.at[pl.ds(tile * rows + chunk * FFN_LOAD_ROWS, FFN_LOAD_ROWS)]
    return pltpu.make_async_copy(src, o_ref.at[pl.ds(chunk * FFN_LOAD_ROWS, FFN_LOAD_ROWS)],
                                 sems.at[chunk])


def _ffn_kernel(*refs, n_side, final_norm):
    x_hbm, g_ref, wg_ref, wu_ref, wd_ref = refs[:5]
    n_in = 6 if final_norm else 5
    side_in = refs[n_in:n_in + n_side]
    o_ref = refs[n_in + n_side]
    side_out = refs[n_in + n_side + 1:n_in + 2 * n_side + 1]
    h_ref, sems = refs[n_in + 2 * n_side + 1:]
    bm, d = o_ref.shape
    n_chunks = bm // FFN_LOAD_ROWS

    @pl.when(pl.program_id(1) == 0)
    def _():
        for c in range(n_chunks):
            _x_chunk_copy(x_hbm, o_ref, sems, pl.program_id(0), c).start()
        for c in range(n_chunks):
            _x_chunk_copy(x_hbm, o_ref, sems, pl.program_id(0), c).wait()
            rows = pl.ds(c * FFN_LOAD_ROWS, FFN_LOAD_ROWS)
            h_ref[rows, :] = _rms_rows(o_ref[rows, :], g_ref[...]).astype(BF16)

    h = h_ref[...]
    gate = jnp.dot(h, wg_ref[...], preferred_element_type=F32)
    up = jnp.dot(h, wu_ref[...], preferred_element_type=F32)
    act = (gate * jax.nn.sigmoid(gate) * up * 0.5).astype(BF16)
    step = min(d, FFN_DOWN_COLS)
    for c0 in range(0, d, step):
        cols = slice(c0, c0 + step)
        o_ref[:, cols] += jnp.dot(act, wd_ref[:, cols], preferred_element_type=F32)

    for src, dst in zip(side_in, side_out):
        dst[...] = src[...].astype(dst.dtype)

    if final_norm:
        @pl.when(pl.program_id(1) == pl.num_programs(1) - 1)
        def _():
            for c in range(n_chunks):
                rows = pl.ds(c * FFN_LOAD_ROWS, FFN_LOAD_ROWS)
                o_ref[rows, :] = _rms_rows(o_ref[rows, :], refs[5][...])


def _side_cast_specs(w, layer, n_steps, step_of):
    _, r, c = w.shape
    rb = 16
    while r // rb > n_steps:
        rb *= 2
    assert r % rb == 0
    last = r // rb - 1
    in_spec = pl.BlockSpec((None, rb, c), lambda *g: (layer, jnp.minimum(step_of(*g), last), 0))
    out_spec = pl.BlockSpec((rb, c), lambda *g: (jnp.minimum(step_of(*g), last), 0))
    return in_spec, out_spec, jax.ShapeDtypeStruct((r, c), BF16)


def ffn(x, gain, wg, wu, wd, side=(), final_gain=None):
    t, d = x.shape
    f = wg.shape[1]
    bm = _pick(t, 1024)
    bf = _pick(f, 256)
    assert bm % FFN_LOAD_ROWS == 0 and d % min(d, FFN_DOWN_COLS) == 0
    grid = (t // bm, f // bf)
    side_specs = [_side_cast_specs(w, layer, grid[0] * grid[1], lambda i, j: i * grid[1] + j)
                  for w, layer in side]
    extra = [] if final_gain is None else [final_gain.reshape(1, d)]
    outs = pl.pallas_call(
        functools.partial(_ffn_kernel, n_side=len(side), final_norm=final_gain is not None),
        grid=grid,
        in_specs=[pl.BlockSpec(memory_space=pl.ANY),
                  pl.BlockSpec((1, d), lambda i, j: (0, 0)),
                  pl.BlockSpec((d, bf), lambda i, j: (0, j)),
                  pl.BlockSpec((d, bf), lambda i, j: (0, j)),
                  pl.BlockSpec((bf, d), lambda i, j: (j, 0))]
        + [pl.BlockSpec((1, d), lambda i, j: (0, 0))] * len(extra) + [s[0] for s in side_specs],
        out_specs=[pl.BlockSpec((bm, d), lambda i, j: (i, 0), pipeline_mode=pl.Buffered(1))]
        + [s[1] for s in side_specs],
        out_shape=[jax.ShapeDtypeStruct((t, d), F32)] + [s[2] for s in side_specs],
        scratch_shapes=[pltpu.VMEM((bm, d), BF16),
                        pltpu.SemaphoreType.DMA((bm // FFN_LOAD_ROWS,))],
        compiler_params=_params("arbitrary", "arbitrary"),
        name="ffn",
    )(x, gain.reshape(1, d), wg, wu, wd, *extra, *[w for w, _ in side])
    return outs[0], list(outs[1:])


PROJ_CHUNK_COLS = 256


def _col_chunks(n):
    step = min(n, PROJ_CHUNK_COLS)
    return [slice(c0, c0 + step) for c0 in range(0, n, step)]


def _glu_kernel(h_ref, wa_ref, wb_ref, ba_ref, bb_ref, o_ref):
    h = h_ref[...]
    for cols in _col_chunks(o_ref.shape[1]):
        a = jnp.dot(h, wa_ref[:, cols], preferred_element_type=F32) + ba_ref[:, cols]
        b = jnp.dot(h, wb_ref[:, cols], preferred_element_type=F32) + bb_ref[:, cols]
        o_ref[:, cols] = a * jax.nn.sigmoid(b)


def glu_proj(h, w, bias):
    t, d = h.shape
    n = w.shape[1] // 2
    bm = _pick(t, 1024)
    bn = _pick(n, 512)
    nb = n // bn
    bias = bias.reshape(1, 2 * n)
    return pl.pallas_call(
        _glu_kernel,
        grid=(t // bm, nb),
        in_specs=[pl.BlockSpec((bm, d), lambda i, j: (i, 0)),
                  pl.BlockSpec((d, bn), lambda i, j: (0, j)),
                  pl.BlockSpec((d, bn), lambda i, j: (0, j + nb)),
                  pl.BlockSpec((1, bn), lambda i, j: (0, j)),
                  pl.BlockSpec((1, bn), lambda i, j: (0, j + nb))],
        out_specs=pl.BlockSpec((bm, bn), lambda i, j: (i, j)),
        out_shape=jax.ShapeDtypeStruct((t, n), F32),
        compiler_params=_params("parallel", "arbitrary"),
        name="glu_proj",
    )(h, w, w, bias, bias)


def _res_proj_kernel(h_ref, w_ref, b_ref, x_ref, o_ref):
    h = h_ref[...]
    for cols in _col_chunks(o_ref.shape[1]):
        acc = jnp.dot(h, w_ref[:, cols], preferred_element_type=F32)
        o_ref[:, cols] = x_ref[:, cols] + (acc + b_ref[:, cols])


def res_proj(h, w, bias, x):
    t, k = h.shape
    n = w.shape[1]
    bm = _pick(t, 1024)
    bn = _pick(n, 512)
    return pl.pallas_call(
        _res_proj_kernel,
        grid=(t // bm, n // bn),
        in_specs=[pl.BlockSpec((bm, k), lambda i, j: (i, 0)),
                  pl.BlockSpec((k, bn), lambda i, j: (0, j)),
                  pl.BlockSpec((1, bn), lambda i, j: (0, j)),
                  pl.BlockSpec((bm, bn), lambda i, j: (i, j))],
        out_specs=pl.BlockSpec((bm, bn), lambda i, j: (i, j)),
        out_shape=jax.ShapeDtypeStruct((t, n), F32),
        compiler_params=_params("parallel", "arbitrary"),
        name="res_proj",
    )(h, w, bias.reshape(1, n), x)


def _qkv_kernel(*refs, tiles_per_part, n_side):
    h_ref, w_ref, cos_ref, sin_ref = refs[:4]
    side_in = refs[4:4 + n_side]
    o_ref = refs[4 + n_side]
    side_out = refs[5 + n_side:5 + 2 * n_side]
    for src, dst in zip(side_in, side_out):
        dst[...] = src[...].astype(dst.dtype)
    h = h_ref[...]
    classes, rows, bn = o_ref.shape
    rotate = pl.program_id(1) // tiles_per_part < 2
    cos = jnp.where(rotate, cos_ref[...], 1.0)
    sin = jnp.where(rotate, sin_ref[...], 0.0)
    for chunk in _col_chunks(bn):
        acc = jnp.dot(h, w_ref[:, chunk], preferred_element_type=F32)
        for off in range(0, chunk.stop - chunk.start, HEAD_DIM):
            t = acc[:, off:off + HEAD_DIM]
            val = (t * cos + pltpu.roll(t, HEAD_DIM // 2, axis=1) * sin).astype(o_ref.dtype)
            cols = slice(chunk.start + off, chunk.start + off + HEAD_DIM)
            for c in range(classes):
                o_ref[c, :, cols] = val[c * rows:(c + 1) * rows]


def qkv_proj(h, w, group, dilation, cos_full, sin_signed, batch, seq, part_width, side=()):
    t, d = h.shape
    n = 3 * part_width
    bm = _pick(seq, 1024)
    bn = _pick(part_width, 1024)
    pos_tiles = seq // bm
    col_tiles = n // bn
    length = seq // dilation
    classes, rows = max(1, bm // length), min(bm, length)
    tiles_per_class = length // rows

    def out_index(i, j):
        tile = i % pos_tiles
        return (i // pos_tiles, tile // tiles_per_class, tile % tiles_per_class, j)

    kern = functools.partial(_qkv_kernel, tiles_per_part=part_width // bn, n_side=len(side))
    grid = (t // bm, col_tiles)
    side_specs = [_side_cast_specs(sw, layer, grid[0] * grid[1], lambda i, j: i * col_tiles + j)
                  for sw, layer in side]
    outs = pl.pallas_call(
        kern,
        grid=grid,
        in_specs=[pl.BlockSpec((bm, d), lambda i, j: (i, 0), pipeline_mode=pl.Buffered(1)),
                  pl.BlockSpec((d, bn), lambda i, j: (0, group * col_tiles + j)),
                  pl.BlockSpec((bm, HEAD_DIM), lambda i, j: (i % pos_tiles, 0)),
                  pl.BlockSpec((bm, HEAD_DIM), lambda i, j: (i % pos_tiles, 0))]
        + [s[0] for s in side_specs],
        out_specs=[pl.BlockSpec((None, classes, rows, bn), out_index)] + [s[1] for s in side_specs],
        out_shape=[jax.ShapeDtypeStruct((batch, dilation, length, n), BF16)]
        + [s[2] for s in side_specs],
        compiler_params=_params("arbitrary", "arbitrary"),
        name=f"qkv_proj_d{dilation}",
    )(h, w, cos_full, sin_signed, *[sw for sw, _ in side])
    return outs[0], list(outs[1:])


CONV_HALO = 16
CONV_TILE_ROWS = 512
CONV_ROWS = 64
CONV_COLS = 256


def _conv_kernel(prev_ref, cur_ref, next_ref, dw_ref, dwb_ref, lng_ref, lnb_ref,
                 o_ref, xs_ref, cv_ref, sh_ref):
    ts, d = cur_ref.shape[1], cur_ref.shape[2]
    i = pl.program_id(1)
    last = pl.num_programs(1) - 1
    xs_ref[0:CONV_HALO, :] = jnp.where(i > 0, prev_ref[0], 0.0)
    xs_ref[CONV_HALO:CONV_HALO + ts, :] = cur_ref[0]
    xs_ref[CONV_HALO + ts:, :] = jnp.where(i < last, next_ref[0], 0.0)

    n_rows = ts // CONV_ROWS
    n_cols = d // CONV_COLS
    shift = CONV_HALO - CONV_WIDTH // 2

    span = sh_ref.shape[1]

    def chunk(idx, carry):
        cols = pl.ds(pl.multiple_of(idx * CONV_COLS, CONV_COLS), CONV_COLS)
        for b in range(V7X_SUBLANES):
            sh_ref[b] = xs_ref[b:b + span, cols]
        for r0 in range(0, ts, CONV_ROWS):
            acc = jnp.zeros((CONV_ROWS, CONV_COLS), F32)
            for k in range(CONV_WIDTH):
                a, b = divmod(k + shift, V7X_SUBLANES)
                row = r0 + a * V7X_SUBLANES
                acc = acc + sh_ref[b, row:row + CONV_ROWS, :] * dw_ref[k:k + 1, cols]
            cv_ref[r0:r0 + CONV_ROWS, cols] = acc + dwb_ref[:, cols]
        return carry

    lax.fori_loop(0, n_cols, chunk, 0)

    def norm(r, carry):
        r0 = pl.multiple_of(r * CONV_ROWS, CONV_ROWS)
        hf = cv_ref[pl.ds(r0, CONV_ROWS), :]
        mu = jnp.mean(hf, axis=-1, keepdims=True)
        cen = hf - mu
        var = jnp.mean(cen * cen, axis=-1, keepdims=True)
        y = cen * lax.rsqrt(var + LN_EPS) * lng_ref[...] + lnb_ref[...]
        o_ref[0, pl.ds(r0, CONV_ROWS), :] = (y * jax.nn.sigmoid(y)).astype(o_ref.dtype)
        return carry

    lax.fori_loop(0, n_rows, norm, 0)


def conv_core(x, dw, dw_b, ln_g, ln_b):
    b, s, d = x.shape
    ts = _pick(s, CONV_TILE_ROWS)
    hb = ts // CONV_HALO
    n_halo_blocks = s // CONV_HALO
    span_extra = (CONV_WIDTH - 1 + CONV_HALO - CONV_WIDTH // 2) // V7X_SUBLANES * V7X_SUBLANES
    assert span_extra + V7X_SUBLANES <= 2 * CONV_HALO
    row = lambda v: v.reshape(1, d)
    return pl.pallas_call(
        _conv_kernel,
        grid=(b, s // ts),
        in_specs=[
            pl.BlockSpec((1, CONV_HALO, d), lambda bi, i: (bi, jnp.maximum(i * hb - 1, 0), 0)),
            pl.BlockSpec((1, ts, d), lambda bi, i: (bi, i, 0)),
            pl.BlockSpec((1, CONV_HALO, d),
                         lambda bi, i: (bi, jnp.minimum((i + 1) * hb, n_halo_blocks - 1), 0)),
            pl.BlockSpec((CONV_WIDTH, d), lambda bi, i: (0, 0)),
            pl.BlockSpec((1, d), lambda bi, i: (0, 0)),
            pl.BlockSpec((1, d), lambda bi, i: (0, 0)),
            pl.BlockSpec((1, d), lambda bi, i: (0, 0)),
        ],
        out_specs=pl.BlockSpec((1, ts, d), lambda bi, i: (bi, i, 0)),
        out_shape=jax.ShapeDtypeStruct((b, s, d), BF16),
        scratch_shapes=[pltpu.VMEM((ts + 2 * CONV_HALO, d), F32),
                        pltpu.VMEM((ts, d), F32),
                        pltpu.VMEM((V7X_SUBLANES, ts + span_extra, CONV_COLS), F32)],
        compiler_params=_params("parallel", "arbitrary"),
        name="conv_core",
    )(x, x, x, dw, row(dw_b), row(ln_g), row(ln_b))


ATTN_HEADS_PER_STEP = 8
ATTN_Q_ROWS = 512
ATTN_SUB_ROWS = 128


def _scatter_classes(src_ref, dst_ref, dilation):
    rows = dst_ref.shape[1]
    for c in range(dilation):
        for s in range(dst_ref.shape[0]):
            lanes = slice(s * V7X_LANES, (s + 1) * V7X_LANES)
            dst_ref[s, pl.ds(c, rows // dilation, stride=dilation), :] = src_ref[c, :, lanes]


def _band_attn_kernel(*refs, radius, length, merge_dilations):
    q_ref, kp_ref, kc_ref, kn_ref, vp_ref, vc_ref, vn_ref = refs[:7]
    n_merge = len(merge_dilations)
    merge_o, merge_l = refs[7:7 + n_merge], refs[7 + n_merge:7 + 2 * n_merge]
    if n_merge:
        out_ref, scr_ref = refs[7 + 2 * n_merge:]
        for g, dil in enumerate(merge_dilations):
            _scatter_classes(merge_o[g], scr_ref.at[2 * g], dil)
            _scatter_classes(merge_l[g], scr_ref.at[2 * g + 1], dil)
    else:
        o_ref, lse_ref = refs[7:]
    bq = q_ref.shape[0]
    sub = min(bq, ATTN_SUB_ROWS)
    nk = sub + 2 * radius
    j0 = pl.program_id(3) * bq
    qi = lax.broadcasted_iota(jnp.int32, (sub, nk), 0)
    kj = lax.broadcasted_iota(jnp.int32, (sub, nk), 1) - radius
    band = jnp.abs(kj - qi) <= radius
    scale = HEAD_DIM ** -0.5
    for r0 in range(0, bq, sub):
        kabs = kj + (j0 + r0)
        valid = band & (kabs >= 0) & (kabs < length)
        for hh in range(q_ref.shape[1] // HEAD_DIM):
            cols = slice(hh * HEAD_DIM, (hh + 1) * HEAD_DIM)
            q = q_ref[r0:r0 + sub, cols]
            parts = []
            for src, lo in ((kp_ref, vp_ref), 0), ((kc_ref, vc_ref), radius), ((kn_ref, vn_ref), radius + bq):
                a, b = max(r0, lo), min(r0 + nk, lo + src[0].shape[0])
                if a < b:
                    parts.append((src, a - lo, b - lo))
            keys = jnp.concatenate([s[0][a:b, cols] for s, a, b in parts], axis=0)
            vals = jnp.concatenate([s[1][a:b, cols] for s, a, b in parts], axis=0)
            sc = lax.dot_general(q, keys, (((1,), (1,)), ((), ())),
                                 preferred_element_type=F32) * scale
            sc = jnp.where(valid, sc, NEG_INF)
            m = jnp.max(sc, axis=-1, keepdims=True)
            p = jnp.exp(sc - m)
            den = jnp.sum(p, axis=-1, keepdims=True)
            o = jnp.dot(p.astype(BF16), vals, preferred_element_type=F32) / den
            lse = jnp.broadcast_to(m + jnp.log(den), (sub, HEAD_DIM))
            if not n_merge:
                o_ref[r0:r0 + sub, cols] = o
                lse_ref[r0:r0 + sub, cols] = lse
                continue
            outs = [o] + [scr_ref[2 * g, hh, r0:r0 + sub, :] for g in range(n_merge)]
            lses = [lse] + [scr_ref[2 * g + 1, hh, r0:r0 + sub, :] for g in range(n_merge)]
            top = functools.reduce(jnp.maximum, lses)
            ws = [jnp.exp(l - top) for l in lses]
            total = functools.reduce(lambda x, y: x + y, ws)
            mixed = functools.reduce(lambda x, y: x + y, [w * v for w, v in zip(ws, outs)])
            out_ref[r0:r0 + sub, cols] = (mixed / total).astype(out_ref.dtype)


def band_attention(qkv, dilation, radius, merge=None):
    batch, _, length, total = qkv.shape
    width = total // 3
    w = min(ATTN_HEADS_PER_STEP * HEAD_DIM, width)
    bq = _pick(length, ATTN_Q_ROWS)
    assert bq % radius == 0 and length % radius == 0
    rb = bq // radius
    n_rblocks = length // radius
    hblocks = width // w

    def cur(part):
        return pl.BlockSpec((None, None, bq, w), lambda b, c, h, i: (b, c, i, part * hblocks + h))

    def prev(part):
        return pl.BlockSpec((None, None, radius, w),
                            lambda b, c, h, i: (b, c, jnp.maximum(i * rb - 1, 0), part * hblocks + h))

    def nxt(part):
        return pl.BlockSpec(
            (None, None, radius, w),
            lambda b, c, h, i: (b, c, jnp.minimum((i + 1) * rb, n_rblocks - 1), part * hblocks + h))

    in_specs = [cur(0), prev(1), cur(1), nxt(1), prev(2), cur(2), nxt(2)]
    operands = [qkv] * 7
    if merge is None:
        merge_dilations = ()
        out_spec = pl.BlockSpec((None, None, bq, w), lambda b, c, h, i: (b, c, i, h))
        out_specs = [out_spec, out_spec]
        out_shape = [jax.ShapeDtypeStruct((batch, dilation, length, width), F32)] * 2
        scratch = []
    else:
        assert dilation == 1
        merge_dilations = tuple(o.shape[1] for o in merge[0])
        tiles = length // bq
        for dil in merge_dilations:
            in_specs.append(pl.BlockSpec((None, dil, bq // dil, w), lambda b, c, h, i: (b, 0, i, h)))
        in_specs += in_specs[7:]
        operands += list(merge[0]) + list(merge[1])
        out_specs = pl.BlockSpec((bq, w), lambda b, c, h, i: (b * tiles + i, h))
        out_shape = jax.ShapeDtypeStruct((batch * length, width), BF16)
        scratch = [pltpu.VMEM((2 * len(merge_dilations), w // V7X_LANES, bq, V7X_LANES), F32)]
    kern = functools.partial(_band_attn_kernel, radius=radius, length=length,
                             merge_dilations=merge_dilations)
    return pl.pallas_call(
        kern,
        grid=(batch, dilation, hblocks, length // bq),
        in_specs=in_specs,
        out_specs=out_specs,
        out_shape=out_shape,
        scratch_shapes=scratch,
        compiler_params=_params("parallel", "parallel", "parallel", "arbitrary"),
        name=f"band_attn_d{dilation}",
    )(*operands)


CAST_BLOCK_BYTES = 4 * 1024 * 1024


def _cast_kernel(w_ref, o_ref):
    o_ref[...] = w_ref[...].astype(o_ref.dtype)


def cast_layer(w, layer):
    _, r, c = w.shape
    br = _pick(r, max(16, CAST_BLOCK_BYTES // (4 * c)))
    return pl.pallas_call(
        _cast_kernel,
        grid=(r // br,),
        in_specs=[pl.BlockSpec((None, br, c), lambda i: (layer, i, 0))],
        out_specs=pl.BlockSpec((br, c), lambda i: (i, 0)),
        out_shape=jax.ShapeDtypeStruct((r, c), BF16),
        compiler_params=_params("parallel"),
        name="cast_layer",
    )(w)


def _rope_tables(seq):
    inv = ROPE_THETA ** (-jnp.arange(0, HEAD_DIM, 2, dtype=F32) / HEAD_DIM)
    ang = jnp.arange(seq, dtype=F32)[:, None] * inv[None, :]
    cos, sin = jnp.cos(ang), jnp.sin(ang)
    return jnp.concatenate([cos, cos], axis=-1), jnp.concatenate([-sin, sin], axis=-1)


def kernel(x, ffn1_norm, ffn1_wg, ffn1_wu, ffn1_wd, mix_norm, ffn2_norm, ffn2_wg, ffn2_wu, ffn2_wd, conv_pw1, conv_pw1_b, conv_dw, conv_dw_b, conv_ln_g, conv_ln_b, conv_pw2, conv_pw2_b, attn_wqkv, attn_wo, final_norm):
    batch, seq, d = x.shape
    assert ffn1_norm.shape[0] == 2 and conv_pw1.shape[0] == 1 and attn_wqkv.shape[0] == 1
    width = attn_wo.shape[1]
    cos_full, sin_signed = _rope_tables(seq)
    xt = x.reshape(batch * seq, d)

    w_ffn = [cast_layer(w, 0) for w in (ffn1_wg, ffn1_wu, ffn1_wd)]
    xt, (pw1, pw2, *w_ffn) = ffn(
        xt, ffn1_norm[0], *w_ffn,
        side=[(conv_pw1, 0), (conv_pw2, 0), (ffn2_wg, 0), (ffn2_wu, 0), (ffn2_wd, 0)])
    h = rmsnorm(xt, mix_norm[0], BF16)
    glu = glu_proj(h, pw1, conv_pw1_b[0])
    act = conv_core(glu.reshape(batch, seq, d), conv_dw[0], conv_dw_b[0], conv_ln_g[0], conv_ln_b[0])
    xt = res_proj(act.reshape(batch * seq, d), pw2, conv_pw2_b[0], xt)
    xt, w_ffn = ffn(xt, ffn2_norm[0], *w_ffn, side=[(ffn1_wg, 1), (ffn1_wu, 1), (ffn1_wd, 1)])

    xt, (wqkv, wo) = ffn(xt, ffn1_norm[1], *w_ffn, side=[(attn_wqkv, 0), (attn_wo, 0)])
    dilations = [dil for _, dil in ATTN_GROUPS]
    hs = rmsnorm_classes(xt, mix_norm[1], batch, seq, dilations)
    order = sorted(range(len(ATTN_GROUPS)), key=lambda g: -ATTN_GROUPS[g][1])
    assert ATTN_GROUPS[order[-1]][1] == 1
    outs, lses, w_ffn = [], [], []
    for g, w_next in zip(order, (ffn2_wg, ffn2_wu, ffn2_wd)):
        window, dil = ATTN_GROUPS[g]
        by_class = lambda tab: tab.reshape(seq // dil, dil, HEAD_DIM).swapaxes(0, 1).reshape(seq, HEAD_DIM)
        qkv, cast = qkv_proj(hs[g], wqkv, g, dil, by_class(cos_full), by_class(sin_signed),
                             batch, seq, width, side=[(w_next, 1)])
        w_ffn += cast
        if g != order[-1]:
            o, l = band_attention(qkv, dil, window // (2 * dil))
            outs.append(o)
            lses.append(l)
        else:
            mixed = band_attention(qkv, dil, window // (2 * dil), merge=(outs, lses))
    xt = res_proj(mixed, wo, jnp.zeros((d,), F32), xt)
    xt, _ = ffn(xt, ffn2_norm[1], *w_ffn, final_gain=final_norm)
    return xt.reshape(batch, seq, d)
```
